```python
import math
import jax, jax.numpy as jnp
from jax import lax
import numpy as np

D_MODEL = 1024
BATCH = 2
SEQ = 8192
DEPTH = 4
DEC_BATCH = 128
DEC_SEQ = 1
PAST_LEN = 2048
PAGE_SIZE = 128

N_HEADS = 16
HEAD_DIM = D_MODEL // N_HEADS
ATTN_DIM = N_HEADS * HEAD_DIM
MOBA_BLOCK = 256
MOBA_TOPK = 3
Q_BLOCK = 32
CHUNK = 128
GMLP_HALF = 3 * D_MODEL
GMLP_GROUPS = 8
GMLP_GROUP_DIM = GMLP_HALF // GMLP_GROUPS
D_FF = 4 * D_MODEL
N_ATTN_LAYERS = (DEPTH + 1) // 2
N_GMLP_LAYERS = DEPTH // 2
EPS = 1e-6
NEG_INF = -1e30

kernel_name = "moba_gmlp_hybrid_step"


def rmsnorm(x, g):
    xf = x.astype(jnp.float32)
    y = xf * lax.rsqrt(jnp.mean(xf * xf, axis=-1, keepdims=True) + EPS)
    return (y * g.astype(jnp.float32)).astype(x.dtype)


def layernorm(x, g, b):
    xf = x.astype(jnp.float32)
    mu = jnp.mean(xf, axis=-1, keepdims=True)
    xc = xf - mu
    y = xc * lax.rsqrt(jnp.mean(xc * xc, axis=-1, keepdims=True) + EPS)
    return (y * g.astype(jnp.float32) + b.astype(jnp.float32)).astype(x.dtype)


def sqrelu_mlp(h, w_up, w_down):
    return jnp.square(jax.nn.relu(h @ w_up)) @ w_down


def select_blocks(q, kmean, q_block):
    s = jnp.einsum('bqhd,bnhd->bhqn', q.astype(jnp.float32), kmean)
    nb = kmean.shape[1]
    past = jnp.arange(nb)[None, :] < q_block[:, None]
    s = jnp.where(past[None, None], s, NEG_INF)
    _, idx = lax.top_k(s, MOBA_TOPK)
    valid = jnp.arange(MOBA_TOPK)[None, :] < q_block[:, None]
    return idx, valid


def moba_core(q, k_sel, v_sel, valid, k_own, v_own, own_mask):
    scale = HEAD_DIM ** -0.5
    s_sel = jnp.einsum('bqhd,bhqtld->bhqtl', q, k_sel).astype(jnp.float32) * scale
    s_sel = jnp.where(valid[None, None, :, :, None], s_sel, NEG_INF)
    s_own = jnp.einsum('bqhd,blhd->bhql', q, k_own).astype(jnp.float32) * scale
    s_own = jnp.where(own_mask[None, None], s_own, NEG_INF)
    b, h, nq, nt, nl = s_sel.shape
    n_sel = nt * nl
    p = jax.nn.softmax(jnp.concatenate([s_sel.reshape(b, h, nq, n_sel), s_own], axis=-1), axis=-1)
    p = p.astype(v_own.dtype)
    p_sel = p[..., :n_sel].reshape(b, h, nq, nt, nl)
    p_own = p[..., n_sel:]
    return (jnp.einsum('bhqtl,bhqtld->bqhd', p_sel, v_sel)
            + jnp.einsum('bhql,blhd->bqhd', p_own, v_own))


def split_qkv(h, w_qkv):
    b, s, _ = h.shape
    qkv = (h @ w_qkv).reshape(b, s, 3, N_HEADS, HEAD_DIM)
    return qkv[:, :, 0], qkv[:, :, 1], qkv[:, :, 2]


def moba_prompt(h, w_qkv, w_o):
    b, s, _ = h.shape
    q, k, v = split_qkv(h, w_qkv)
    nb = max(-(-s // MOBA_BLOCK), MOBA_TOPK)
    pad = nb * MOBA_BLOCK - s
    kb = jnp.pad(k, ((0, 0), (0, pad), (0, 0), (0, 0))).reshape(b, nb, MOBA_BLOCK, N_HEADS, HEAD_DIM)
    vb = jnp.pad(v, ((0, 0), (0, pad), (0, 0), (0, 0))).reshape(b, nb, MOBA_BLOCK, N_HEADS, HEAD_DIM)
    kmean = jnp.mean(kb, axis=2, dtype=jnp.float32)
    kb_t = kb.transpose(0, 3, 1, 2, 4)
    vb_t = vb.transpose(0, 3, 1, 2, 4)
    gather = jax.vmap(jax.vmap(lambda kbh, ih: kbh[ih]))
    n_qb = s // Q_BLOCK
    q_blocks = q.reshape(b, n_qb, Q_BLOCK, N_HEADS, HEAD_DIM).transpose(1, 0, 2, 3, 4)
    starts = jnp.arange(n_qb, dtype=jnp.int32) * Q_BLOCK

    def one_block(args):
        qc, start = args
        pos = start + jnp.arange(Q_BLOCK, dtype=jnp.int32)
        idx, valid = select_blocks(qc, kmean, pos // MOBA_BLOCK)
        k_sel = gather(kb_t, idx)
        v_sel = gather(vb_t, idx)
        ob = start // MOBA_BLOCK
        k_own = lax.dynamic_index_in_dim(kb, ob, axis=1, keepdims=False)
        v_own = lax.dynamic_index_in_dim(vb, ob, axis=1, keepdims=False)
        own_mask = (ob * MOBA_BLOCK + jnp.arange(MOBA_BLOCK, dtype=jnp.int32))[None, :] <= pos[:, None]
        return moba_core(qc, k_sel, v_sel, valid, k_own, v_own, own_mask)

    o = lax.map(one_block, (q_blocks, starts))
    o = o.transpose(1, 0, 2, 3, 4).reshape(b, s, ATTN_DIM)
    return o @ w_o, k, v


def moba_sample(h, cache_k, cache_v, page_means, page_table, layer, w_qkv, w_o):
    db, ds, _ = h.shape
    q, k, v = split_qkv(h, w_qkv)
    n_pages = page_table.shape[1]
    past_len = n_pages * PAGE_SIZE
    ppb = MOBA_BLOCK // PAGE_SIZE
    ob = past_len // MOBA_BLOCK
    assert (past_len % MOBA_BLOCK) + ds <= MOBA_BLOCK
    nb = max(ob, MOBA_TOPK)
    pm = page_means[page_table[:, :ob * ppb]]
    kmean = pm.reshape(db, ob, ppb, N_HEADS, HEAD_DIM).mean(axis=2)
    kmean = jnp.pad(kmean, ((0, 0), (0, nb - ob), (0, 0), (0, 0)))
    idx, valid = select_blocks(q, kmean, jnp.full((ds,), ob, jnp.int32))
    logical = jnp.minimum(idx[..., None] * ppb + jnp.arange(ppb, dtype=jnp.int32), n_pages - 1)
    phys = page_table[jnp.arange(db)[:, None, None, None, None], logical]
    rows = jnp.arange(PAGE_SIZE)
    heads = jnp.arange(N_HEADS)[None, :, None, None, None, None]
    sel_shape = (db, N_HEADS, ds, MOBA_TOPK, MOBA_BLOCK, HEAD_DIM)
    k_sel = cache_k[layer, phys[..., None], rows, heads].reshape(sel_shape)
    v_sel = cache_v[layer, phys[..., None], rows, heads].reshape(sel_shape)
    n_own = (past_len - ob * MOBA_BLOCK) // PAGE_SIZE
    own_pages = page_table[:, ob * ppb: ob * ppb + n_own]
    k_own = jnp.concatenate(
        [cache_k[layer, own_pages].reshape(db, n_own * PAGE_SIZE, N_HEADS, HEAD_DIM), k], axis=1)
    v_own = jnp.concatenate(
        [cache_v[layer, own_pages].reshape(db, n_own * PAGE_SIZE, N_HEADS, HEAD_DIM), v], axis=1)
    own_mask = jnp.concatenate(
        [jnp.ones((ds, n_own * PAGE_SIZE), bool), jnp.tril(jnp.ones((ds, ds), bool))], axis=1)
    o = moba_core(q, k_sel, v_sel, valid, k_own, v_own, own_mask).reshape(db, ds, ATTN_DIM)
    return o @ w_o, k, v


def gmlp_mix(h, w_uv, ln_g, ln_b, w_s, b_s, w_o):
    b, s, _ = h.shape
    z = jax.nn.gelu(h @ w_uv)
    u, v = z[..., :GMLP_HALF], z[..., GMLP_HALF:]
    v = layernorm(v, ln_g, ln_b)
    lc = min(s, CHUNK)
    n_c = s // lc
    vc = v.reshape(b, n_c, lc, GMLP_GROUPS, GMLP_GROUP_DIM)
    ws = jnp.where(jnp.tril(jnp.ones((lc, lc), bool))[None], w_s[:, :lc, :lc], 0.0).astype(v.dtype)
    f = jnp.einsum('gts,bnsgd->bntgd', ws, vc) + b_s[:, :lc].T[None, None, :, :, None]
    return (u * f.reshape(b, s, GMLP_HALF)) @ w_o, v


def setup_inputs(seed: int = 0) -> dict:
    key = jax.random.key(seed)
    ks = jax.random.split(key, 20)
    f32 = jnp.float32
    n_pages = PAST_LEN // PAGE_SIZE
    n_pool = (DEC_BATCH * n_pages * 5) // 4

    def nrm(k, shape, scale):
        return jax.random.normal(k, shape, f32) * scale

    x_prompt = nrm(ks[0], (BATCH, SEQ, D_MODEL), 1.0)
    x_sample = nrm(ks[1], (DEC_BATCH, DEC_SEQ, D_MODEL), 1.0)
    cache_k = nrm(ks[2], (N_ATTN_LAYERS, n_pool, PAGE_SIZE, N_HEADS, HEAD_DIM), 1.0)
    cache_v = nrm(ks[3], (N_ATTN_LAYERS, n_pool, PAGE_SIZE, N_HEADS, HEAD_DIM), 1.0)
    page_table = jax.random.permutation(ks[4], n_pool)[: DEC_BATCH * n_pages].reshape(
        DEC_BATCH, n_pages).astype(jnp.int32)
    norm_mix = 1.0 + nrm(ks[5], (DEPTH, D_MODEL), 0.02)
    norm_ffn = 1.0 + nrm(ks[6], (DEPTH, D_MODEL), 0.02)
    w_qkv = nrm(ks[7], (N_ATTN_LAYERS, D_MODEL, 3 * ATTN_DIM), D_MODEL ** -0.5)
    w_o_attn = nrm(ks[8], (N_ATTN_LAYERS, ATTN_DIM, D_MODEL), ATTN_DIM ** -0.5)
    w_uv = nrm(ks[9], (N_GMLP_LAYERS, D_MODEL, 2 * GMLP_HALF), D_MODEL ** -0.5)
    ln_v_g = 1.0 + nrm(ks[10], (N_GMLP_LAYERS, GMLP_HALF), 0.02)
    ln_v_b = nrm(ks[11], (N_GMLP_LAYERS, GMLP_HALF), 0.02)
    w_s = nrm(ks[12], (N_GMLP_LAYERS, GMLP_GROUPS, CHUNK, CHUNK), CHUNK ** -0.5)
    b_s = 1.0 + nrm(ks[13], (N_GMLP_LAYERS, GMLP_GROUPS, CHUNK), 0.02)
    w_o_gmlp = nrm(ks[14], (N_GMLP_LAYERS, GMLP_HALF, D_MODEL), GMLP_HALF ** -0.5)
    w_up = nrm(ks[15], (DEPTH, D_MODEL, D_FF), D_MODEL ** -0.5)
    w_down = nrm(ks[16], (DEPTH, D_FF, D_MODEL), 0.5 * D_FF ** -0.5)
    norm_final = 1.0 + nrm(ks[17], (D_MODEL,), 0.02)
    return {"x_prompt": x_prompt, "x_sample": x_sample, "cache_k": cache_k, "cache_v": cache_v,
            "page_table": page_table, "norm_mix": norm_mix, "norm_ffn": norm_ffn,
            "w_qkv": w_qkv, "w_o_attn": w_o_attn, "w_uv": w_uv, "ln_v_g": ln_v_g,
            "ln_v_b": ln_v_b, "w_s": w_s, "b_s": b_s, "w_o_gmlp": w_o_gmlp,
            "w_up": w_up, "w_down": w_down, "norm_final": norm_final}


def reference(x_prompt, x_sample, cache_k, cache_v, page_table, norm_mix, norm_ffn,
              w_qkv, w_o_attn, w_uv, ln_v_g, ln_v_b, w_s, b_s, w_o_gmlp,
              w_up, w_down, norm_final):
    yp, ys = x_prompt, x_sample
    page_means = jnp.mean(cache_k, axis=2, dtype=jnp.float32)
    k_p, v_p, k_s, v_s, gv_s = [], [], [], [], []
    for i in range(DEPTH):
        hp = rmsnorm(yp, norm_mix[i])
        hs = rmsnorm(ys, norm_mix[i])
        if i % 2 == 0:
            a = i // 2
            op, kp, vp = moba_prompt(hp, w_qkv[a], w_o_attn[a])
            os_, kn, vn = moba_sample(hs, cache_k, cache_v, page_means[a], page_table, a,
                                      w_qkv[a], w_o_attn[a])
            k_p.append(kp); v_p.append(vp); k_s.append(kn); v_s.append(vn)
        else:
            g = i // 2
            op, _ = gmlp_mix(hp, w_uv[g], ln_v_g[g], ln_v_b[g], w_s[g], b_s[g], w_o_gmlp[g])
            os_, vrows = gmlp_mix(hs, w_uv[g], ln_v_g[g], ln_v_b[g], w_s[g], b_s[g], w_o_gmlp[g])
            gv_s.append(vrows)
        yp = yp + op
        ys = ys + os_
        yp = yp + sqrelu_mlp(rmsnorm(yp, norm_ffn[i]), w_up[i], w_down[i])
        ys = ys + sqrelu_mlp(rmsnorm(ys, norm_ffn[i]), w_up[i], w_down[i])
    y_prompt = rmsnorm(yp, norm_final)
    y_sample = rmsnorm(ys, norm_final)
    k_prompt = jnp.stack(k_p)
    v_prompt = jnp.stack(v_p)
    k_sample = jnp.stack(k_s)
    v_sample = jnp.stack(v_s)
    gmlp_v_sample = jnp.stack(gv_s)
    return (y_prompt, y_sample, k_prompt, v_prompt, k_sample, v_sample, gmlp_v_sample)
```

```python
import functools

import jax
import jax.numpy as jnp
from jax import lax
from jax.experimental import pallas as pl
from jax.experimental.pallas import tpu as pltpu

F32 = jnp.float32
BF16 = jnp.bfloat16

N_HEADS = 16
HEAD_DIM = 64
MOBA_BLOCK = 256
MOBA_TOPK = 3
CHUNK = 128
GMLP_GROUPS = 8
PAGE_SIZE = 128
EPS = 1e-6
NEG_BIG = -1e30
LANES = 128
SCALE = HEAD_DIM ** -0.5

VMEM_LIMIT = 56 * 1024 * 1024


def _params(*sem):
    return pltpu.CompilerParams(dimension_semantics=sem, vmem_limit_bytes=VMEM_LIMIT)


def _resident(shape):
    nd = len(shape)
    return pl.BlockSpec(shape, lambda *_: (0,) * nd, pipeline_mode=pl.Buffered(1))


def _rms(x, g):
    return x * lax.rsqrt(jnp.mean(x * x, axis=-1, keepdims=True) + EPS) * g


def _dot(a, b):
    return jnp.dot(a, b, preferred_element_type=F32)


def _dot_nt(a, b, precision=None):
    return lax.dot_general(a, b, (((1,), (1,)), ((), ())), precision=precision,
                           preferred_element_type=F32)


def _qkv_kernel(x_ref, g_ref, w_ref, q_ref, k_ref, v_ref, *, nc):
    d = x_ref.shape[1]
    h = _rms(x_ref[...], g_ref[...]).astype(BF16)
    for j, o_ref in enumerate((q_ref, k_ref, v_ref)):
        for c in range(d // nc):
            o_ref[:, c * nc:(c + 1) * nc] = _dot(h, w_ref[:, j * d + c * nc: j * d + (c + 1) * nc])


def _qkv(x, g, w, tm):
    rows, d = x.shape
    out = jax.ShapeDtypeStruct((rows, d), F32)
    row_spec = pl.BlockSpec((tm, d), lambda i: (i, 0))
    return pl.pallas_call(
        functools.partial(_qkv_kernel, nc=512),
        out_shape=(out, out, out),
        grid=(rows // tm,),
        in_specs=[row_spec, _resident((1, d)), _resident(w.shape)],
        out_specs=(row_spec, row_spec, row_spec),
        compiler_params=_params("parallel"),
        name="qkv",
    )(x, g, w)


def _top3_bias(s, valid, lane):
    s = jnp.where(valid, s, -jnp.inf)
    bias = jnp.full(s.shape, NEG_BIG, F32)
    for _ in range(MOBA_TOPK):
        mx = jnp.max(s, axis=-1, keepdims=True)
        first = jnp.min(jnp.where(s == mx, lane, 1 << 20), axis=-1, keepdims=True)
        pick = lane == first
        bias = jnp.where(pick, 0.0, bias)
        s = jnp.where(pick, -jnp.inf, s)
    return jnp.where(valid, bias, NEG_BIG)


def _moba_kernel(q_ref, k_ref, v_ref, o_ref, ka0_ref, ka1_ref, vb_ref, kmt_ref, *, nb):
    tq = q_ref.shape[0]
    blk = MOBA_BLOCK
    half = HEAD_DIM
    qi = pl.program_id(2)
    lane1 = lax.broadcasted_iota(jnp.int32, (1, LANES), 1)

    @pl.when(qi == 0)
    def _():
        kmt_ref[...] = jnp.zeros(kmt_ref.shape, F32)
        lane_b = lax.broadcasted_iota(jnp.int32, (blk, LANES), 1)

        def build(n, carry):
            rows = pl.ds(pl.multiple_of(n * blk, blk), blk)
            kb = k_ref[rows, :]
            ka0_ref[rows, :] = jnp.where(lane_b < half, kb, (lane_b - half == n).astype(F32)).astype(BF16)
            ka1_ref[rows, :] = jnp.where(lane_b >= half, kb, (lane_b == n).astype(F32)).astype(BF16)
            vb_ref[rows, :] = v_ref[rows, :].astype(BF16)
            km = jnp.mean(kb, axis=0, keepdims=True)
            kmt_ref[pl.ds(n, 1), :] = jnp.where(lane1 >= half, km, 0.0)
            kmt_ref[pl.ds(half + n, 1), :] = jnp.where(lane1 < half, km, 0.0)
            return carry

        lax.fori_loop(0, nb, build, 0)

    q2 = q_ref[...]
    lane = lax.broadcasted_iota(jnp.int32, (tq, LANES), 1)
    s_sel = _dot_nt(q2, kmt_ref[...], precision=lax.Precision.HIGHEST)
    bias0 = _top3_bias(s_sel, (lane >= half) & (lane < half + qi), lane)
    bias1 = _top3_bias(s_sel, lane < qi, lane)

    qs = q2 * SCALE
    in0 = lane < half
    q_own = (jnp.where(in0, qs, 0.0).astype(BF16), jnp.where(in0, 0.0, qs).astype(BF16))
    q_past = (jnp.where(in0, qs, bias0).astype(BF16), jnp.where(in0, bias1, qs).astype(BF16))
    ka_refs = (ka0_ref, ka1_ref)

    own_rows = pl.ds(pl.multiple_of(qi * blk, blk), blk)
    causal = (lax.broadcasted_iota(jnp.int32, (tq, blk), 0) >= lax.broadcasted_iota(jnp.int32, (tq, blk), 1))
    v_own = vb_ref[own_rows, :]
    state = []
    for hh in range(2):
        s = jnp.where(causal, _dot_nt(q_own[hh], ka_refs[hh][own_rows, :]), NEG_BIG)
        m = jnp.max(s, axis=-1, keepdims=True)
        p = jnp.exp(s - m)
        state += [m, jnp.sum(p, axis=-1, keepdims=True), _dot(p.astype(BF16), v_own)]

    def step(n, carry):
        rows = pl.ds(pl.multiple_of(n * blk, blk), blk)
        vb = vb_ref[rows, :]
        out = []
        for hh in range(2):
            m, l, acc = carry[3 * hh: 3 * hh + 3]
            s = _dot_nt(q_past[hh], ka_refs[hh][rows, :])
            m_new = jnp.maximum(m, jnp.max(s, axis=-1, keepdims=True))
            alpha = jnp.exp(m - m_new)
            p = jnp.exp(s - m_new)
            out += [m_new, alpha * l + jnp.sum(p, axis=-1, keepdims=True),
                    alpha * acc + _dot(p.astype(BF16), vb)]
        return tuple(out)

    m0, l0, acc0, m1, l1, acc1 = lax.fori_loop(0, qi, step, tuple(state))
    o_ref[...] = jnp.where(in0, acc0 / l0, acc1 / l1).astype(o_ref.dtype)


def _moba_prompt(q, k, v, batch, seq):
    rows, d = q.shape
    nb = seq // MOBA_BLOCK
    tq = MOBA_BLOCK
    n_pair = d // LANES
    kv_spec = pl.BlockSpec((seq, LANES), lambda b, hp, qi: (b, hp))
    q_spec = pl.BlockSpec((tq, LANES), lambda b, hp, qi: (b * nb + qi, hp))
    return pl.pallas_call(
        functools.partial(_moba_kernel, nb=nb),
        out_shape=jax.ShapeDtypeStruct((rows, d), BF16),
        grid=(batch, n_pair, nb),
        in_specs=[q_spec, kv_spec, kv_spec],
        out_specs=q_spec,
        scratch_shapes=[pltpu.VMEM((seq, LANES), BF16), pltpu.VMEM((seq, LANES), BF16),
                        pltpu.VMEM((seq, LANES), BF16), pltpu.VMEM((LANES, LANES), F32)],
        compiler_params=_params("parallel", "parallel", "arbitrary"),
        name="moba_prompt",
    )(q, k, v)


def _page_mean_kernel(pt_ref, kp_ref, o_ref, *, pages_per_block):
    del pt_ref
    p = pl.program_id(1)
    pm = jnp.mean(kp_ref[0, 0], axis=0) * (1.0 / pages_per_block)
    n = p // pages_per_block

    @pl.when(p % pages_per_block == 0)
    def _():
        o_ref[0, pl.ds(n, 1)] = pm[None]

    @pl.when(p % pages_per_block != 0)
    def _():
        o_ref[0, pl.ds(n, 1)] = o_ref[0, pl.ds(n, 1)] + pm[None]


def _block_means(cache_k, page_table, layer):
    db, n_pages = page_table.shape
    _, _, page, nh, hd = cache_k.shape
    ppb = MOBA_BLOCK // page
    grid_spec = pltpu.PrefetchScalarGridSpec(
        num_scalar_prefetch=1,
        grid=(db, n_pages),
        in_specs=[pl.BlockSpec((1, 1, page, nh, hd), lambda b, p, pt: (layer, pt[b, p], 0, 0, 0))],
        out_specs=pl.BlockSpec((1, n_pages // ppb, nh, hd), lambda b, p, pt: (b, 0, 0, 0)),
    )
    return pl.pallas_call(
        functools.partial(_page_mean_kernel, pages_per_block=ppb),
        out_shape=jax.ShapeDtypeStruct((db, n_pages // ppb, nh, hd), F32),
        grid_spec=grid_spec,
        compiler_params=_params("parallel", "arbitrary"),
        name="decode_block_means",
    )(page_table, cache_k)


def _decode_kernel(pt_ref, q_ref, km_ref, kn_ref, vn_ref, kp_ref, vp_ref, o_ref,
                   bias_ref, m_ref, l_ref, acc_ref, *, pages_per_block):
    del pt_ref
    p = pl.program_id(1)
    q = q_ref[0]

    @pl.when(p == 0)
    def _():
        km = km_ref[0]
        sc = jnp.sum(km * q[None], axis=-1, keepdims=True)
        nidx = lax.broadcasted_iota(jnp.int32, sc.shape, 0)
        bias = jnp.full(sc.shape, NEG_BIG, F32)
        for _ in range(MOBA_TOPK):
            mx = jnp.max(sc, axis=0, keepdims=True)
            first = jnp.min(jnp.where(sc == mx, nidx, 1 << 20), axis=0, keepdims=True)
            pick = nidx == first
            bias = jnp.where(pick, 0.0, bias)
            sc = jnp.where(pick, -jnp.inf, sc)
        bias_ref[...] = jnp.broadcast_to(bias, bias_ref.shape)
        s_own = jnp.sum(q * kn_ref[0], axis=-1, keepdims=True) * SCALE
        m_ref[...] = jnp.broadcast_to(s_own, m_ref.shape)
        l_ref[...] = jnp.ones(l_ref.shape, F32)
        acc_ref[...] = vn_ref[0]

    k = kp_ref[0, 0]
    v = vp_ref[0, 0]
    bias = bias_ref[pl.ds(p // pages_per_block, 1)]
    s = jnp.sum(k * q[None], axis=-1, keepdims=True) * SCALE + bias
    m_old = m_ref[...]
    m_new = jnp.maximum(m_old, jnp.max(s, axis=0))
    alpha = jnp.exp(m_old - m_new)
    pe = jnp.exp(s - m_new[None])
    l_ref[...] = alpha * l_ref[...] + jnp.sum(pe, axis=0)
    acc_ref[...] = alpha * acc_ref[...] + jnp.sum(pe * v, axis=0)
    m_ref[...] = m_new

    @pl.when(p == pl.num_programs(1) - 1)
    def _():
        o_ref[0] = acc_ref[...] / l_ref[...]


def _moba_decode(q, k_new, v_new, kmean, cache_k, cache_v, page_table, layer):
    db, n_pages = page_table.shape
    _, _, page, nh, hd = cache_k.shape
    nbk = kmean.shape[1]
    ppb = MOBA_BLOCK // page
    tok = pl.BlockSpec((1, nh, hd), lambda b, p, pt: (b, 0, 0))
    pg = pl.BlockSpec((1, 1, page, nh, hd), lambda b, p, pt: (layer, pt[b, p], 0, 0, 0))
    grid_spec = pltpu.PrefetchScalarGridSpec(
        num_scalar_prefetch=1,
        grid=(db, n_pages),
        in_specs=[tok, pl.BlockSpec((1, nbk, nh, hd), lambda b, p, pt: (b, 0, 0, 0)), tok, tok, pg, pg],
        out_specs=tok,
        scratch_shapes=[pltpu.VMEM((nbk, nh, hd), F32), pltpu.VMEM((nh, hd), F32),
                        pltpu.VMEM((nh, hd), F32), pltpu.VMEM((nh, hd), F32)],
    )
    return pl.pallas_call(
        functools.partial(_decode_kernel, pages_per_block=ppb),
        out_shape=jax.ShapeDtypeStruct((db, nh, hd), F32),
        grid_spec=grid_spec,
        compiler_params=_params("parallel", "arbitrary"),
        name="moba_decode",
    )(page_table, q, kmean, k_new, v_new, cache_k, cache_v)


def _gmlp_kernel(y_ref, g_ref, wuv_ref, lng_ref, lnb_ref, ws_ref, bs_ref, wo_ref, o_ref, *rest,
                 single_pos, vc):
    if single_pos:
        vout_ref, vbuf = rest
    else:
        (vbuf,) = rest
    tm = y_ref.shape[0]
    half = vbuf.shape[1]
    gd = half // GMLP_GROUPS
    y = y_ref[...]
    h = _rms(y, g_ref[...]).astype(BF16)
    for c in range(half // vc):
        vbuf[:, c * vc:(c + 1) * vc] = jax.nn.gelu(_dot(h, wuv_ref[:, half + c * vc: half + (c + 1) * vc]))
    v = vbuf[...]
    xc = v - jnp.mean(v, axis=-1, keepdims=True)
    vn = xc * lax.rsqrt(jnp.mean(xc * xc, axis=-1, keepdims=True) + EPS) * lng_ref[...] + lnb_ref[...]
    vbuf[...] = vn
    if single_pos:
        vout_ref[...] = vn
    else:
        tril = (lax.broadcasted_iota(jnp.int32, (CHUNK, CHUNK), 0)
                >= lax.broadcasted_iota(jnp.int32, (CHUNK, CHUNK), 1))
    acc = y
    for g in range(GMLP_GROUPS):
        cols = slice(g * gd, (g + 1) * gd)
        u = jax.nn.gelu(_dot(h, wuv_ref[:, cols]))
        vg = vbuf[:, cols]
        if single_pos:
            f = ws_ref[g, 0:1, 0:1] * vg + bs_ref[g, 0:1, :]
        else:
            wsg = jnp.where(tril, ws_ref[g], 0.0).astype(BF16)
            bsg = bs_ref[g]
            f = jnp.concatenate(
                [_dot(wsg, vg[r * CHUNK:(r + 1) * CHUNK].astype(BF16)) + bsg for r in range(tm // CHUNK)],
                axis=0)
        acc = acc + _dot((u * f).astype(BF16), wo_ref[cols, :])
    o_ref[...] = acc


def _gmlp(y, g, wuv, lng, lnb, ws, bs, wo, tm, single_pos):
    rows, d = y.shape
    half = wo.shape[0]
    row_spec = pl.BlockSpec((tm, d), lambda i: (i, 0))
    v_spec = pl.BlockSpec((tm, half), lambda i: (i, 0))
    out_shape = jax.ShapeDtypeStruct((rows, d), F32)
    out_specs = row_spec
    if single_pos:
        out_shape = (out_shape, jax.ShapeDtypeStruct((rows, half), F32))
        out_specs = (row_spec, v_spec)
    return pl.pallas_call(
        functools.partial(_gmlp_kernel, single_pos=single_pos, vc=512),
        out_shape=out_shape,
        grid=(rows // tm,),
        in_specs=[row_spec, _resident((1, d)), _resident(wuv.shape), _resident((1, half)),
                  _resident((1, half)), _resident(ws.shape), _resident(bs.shape), _resident(wo.shape)],
        out_specs=out_specs,
        scratch_shapes=[pltpu.VMEM((tm, half), F32)],
        compiler_params=_params("parallel"),
        name="gmlp_single" if single_pos else "gmlp",
    )(y, g, wuv, lng, lnb, ws, bs, wo)


def _ffn_kernel(*refs, has_proj, has_final, fc):
    it = iter(refs)
    y_ref = next(it)
    y = y_ref[...]
    if has_proj:
        a_ref, wo_ref = next(it), next(it)
        y = y + _dot(a_ref[...], wo_ref[...])
    g_ref, wup_ref, wdn_ref = next(it), next(it), next(it)
    gf_ref = next(it) if has_final else None
    o_ref = next(it)
    h = _rms(y, g_ref[...]).astype(BF16)
    acc = y
    for c in range(wup_ref.shape[1] // fc):
        u = jnp.square(jnp.maximum(_dot(h, wup_ref[:, c * fc:(c + 1) * fc]), 0.0)).astype(BF16)
        acc = acc + _dot(u, wdn_ref[c * fc:(c + 1) * fc, :])
    if has_final:
        acc = _rms(acc, gf_ref[...])
    o_ref[...] = acc


def _ffn(y, g, wup, wdn, tm, proj=None, final_g=None):
    rows, d = y.shape
    row_spec = pl.BlockSpec((tm, d), lambda i: (i, 0))
    args, specs = [y], [row_spec]
    if proj is not None:
        a, wo = proj
        args += [a, wo]
        specs += [pl.BlockSpec((tm, a.shape[1]), lambda i: (i, 0)), _resident(wo.shape)]
    args += [g, wup, wdn]
    specs += [_resident((1, d)), _resident(wup.shape), _resident(wdn.shape)]
    if final_g is not None:
        args.append(final_g)
        specs.append(_resident((1, d)))
    return pl.pallas_call(
        functools.partial(_ffn_kernel, has_proj=proj is not None, has_final=final_g is not None, fc=512),
        out_shape=jax.ShapeDtypeStruct((rows, d), F32),
        grid=(rows // tm,),
        in_specs=specs,
        out_specs=row_spec,
        compiler_params=_params("parallel"),
        name="ffn",
    )(*args)


def kernel(x_prompt, x_sample, cache_k, cache_v, page_table, norm_mix, norm_ffn, w_qkv, w_o_attn,
           w_uv, ln_v_g, ln_v_b, w_s, b_s, w_o_gmlp, w_up, w_down, norm_final):
    batch, seq, d = x_prompt.shape
    db, ds, _ = x_sample.shape
    depth = norm_mix.shape[0]
    assert ds == 1 and seq % MOBA_BLOCK == 0 and d == N_HEADS * HEAD_DIM
    assert (page_table.shape[1] * PAGE_SIZE) % MOBA_BLOCK == 0 and cache_k.shape[2] == PAGE_SIZE
    tm_p, tm_s = 512, db

    yp = x_prompt.reshape(batch * seq, d)
    ys = x_sample.reshape(db * ds, d)
    w_qkv_b, w_o_attn_b, w_uv_b = w_qkv.astype(BF16), w_o_attn.astype(BF16), w_uv.astype(BF16)
    w_o_gmlp_b, w_up_b, w_down_b = w_o_gmlp.astype(BF16), w_up.astype(BF16), w_down.astype(BF16)
    row = lambda t: t.reshape(1, -1)

    k_p, v_p, k_s, v_s, gv_s = [], [], [], [], []
    for i in range(depth):
        g_mix, g_ffn = row(norm_mix[i]), row(norm_ffn[i])
        final_g = row(norm_final) if i == depth - 1 else None
        if i % 2 == 0:
            a = i // 2
            qp, kp, vp = _qkv(yp, g_mix, w_qkv_b[a], tm_p)
            qs, kn, vn = _qkv(ys, g_mix, w_qkv_b[a], tm_s)
            op = _moba_prompt(qp, kp, vp, batch, seq)
            heads = lambda t: t.reshape(db, N_HEADS, HEAD_DIM)
            kmean = _block_means(cache_k, page_table, a)
            os_ = _moba_decode(heads(qs), heads(kn), heads(vn), kmean, cache_k, cache_v, page_table, a)
            k_p.append(kp.reshape(batch, seq, N_HEADS, HEAD_DIM))
            v_p.append(vp.reshape(batch, seq, N_HEADS, HEAD_DIM))
            k_s.append(kn.reshape(db, ds, N_HEADS, HEAD_DIM))
            v_s.append(vn.reshape(db, ds, N_HEADS, HEAD_DIM))
            yp = _ffn(yp, g_ffn, w_up_b[i], w_down_b[i], tm_p, proj=(op, w_o_attn_b[a]), final_g=final_g)
            ys = _ffn(ys, g_ffn, w_up_b[i], w_down_b[i], tm_s,
                      proj=(os_.reshape(db, d).astype(BF16), w_o_attn_b[a]), final_g=final_g)
        else:
            gi = i // 2
            mix = (w_uv_b[gi], row(ln_v_g[gi]), row(ln_v_b[gi]), w_s[gi], b_s[gi][:, :, None], w_o_gmlp_b[gi])
            yp = _gmlp(yp, g_mix, *mix, tm=256, single_pos=False)
            ys, vrows = _gmlp(ys, g_mix, *mix, tm=tm_s, single_pos=True)
            gv_s.append(vrows.reshape(db, ds, -1))
            yp = _ffn(yp, g_ffn, w_up_b[i], w_down_b[i], tm_p, final_g=final_g)
            ys = _ffn(ys, g_ffn, w_up_b[i], w_down_b[i], tm_s, final_g=final_g)
    return (yp.reshape(batch, seq, d), ys.reshape(db, ds, d), jnp.stack(k_p), jnp.stack(v_p),
            jnp.stack(k_s), jnp.stack(v_s), jnp.stack(gv_s))
```

```python
import functools

import jax
import jax.numpy as jnp
from jax import lax
from jax.experimental import pallas as pl
from jax.experimental.pallas import tpu as pltpu

F32 = jnp.float32
BF16 = jnp.bfloat16

N_HEADS = 16
HEAD_DIM = 64
MOBA_BLOCK = 256
MOBA_TOPK = 3
CHUNK = 128
GMLP_GROUPS = 8
PAGE_SIZE = 128
EPS = 1e-6
NEG_BIG = -1e30
LANES = 128
SCALE = HEAD_DIM ** -0.5
KEY_GROUP = 4

VMEM_LIMIT = 56 * 1024 * 1024


def _params(*sem):
    return pltpu.CompilerParams(dimension_semantics=sem, vmem_limit_bytes=VMEM_LIMIT)


def _resident(shape):
    nd = len(shape)
    return pl.BlockSpec(shape, lambda *_: (0,) * nd, pipeline_mode=pl.Buffered(1))


def _rms(x, g):
    return x * lax.rsqrt(jnp.mean(x * x, axis=-1, keepdims=True) + EPS) * g


def _dot(a, b, precision=None):
    return jnp.dot(a, b, preferred_element_type=F32, precision=precision)


def _dot_nt(a, b):
    return lax.dot_general(a, b, (((1,), (1,)), ((), ())), preferred_element_type=F32)


def _qkv_kernel(x_ref, g_ref, w_ref, q_ref, k_ref, v_ref, *, nc):
    d = x_ref.shape[1]
    h = _rms(x_ref[...], g_ref[...]).astype(BF16)
    for j, o_ref in enumerate((q_ref, k_ref, v_ref)):
        for c in range(d // nc):
            o_ref[:, c * nc:(c + 1) * nc] = _dot(h, w_ref[:, j * d + c * nc: j * d + (c + 1) * nc])


def _qkv(x, g, w, tm):
    rows, d = x.shape
    out = jax.ShapeDtypeStruct((rows, d), F32)
    row_spec = pl.BlockSpec((tm, d), lambda i: (i, 0))
    return pl.pallas_call(
        functools.partial(_qkv_kernel, nc=512),
        out_shape=(out, out, out),
        grid=(rows // tm,),
        in_specs=[row_spec, _resident((1, d)), _resident(w.shape)],
        out_specs=(row_spec, row_spec, row_spec),
        compiler_params=_params("parallel"),
        name="qkv",
    )(x, g, w)


def _qkv_t_kernel(x_ref, g_ref, wq_ref, wkt_ref, wvt_ref, q_ref, kt_ref, vt_ref, *, nc):
    d = x_ref.shape[1]
    h = _rms(x_ref[...], g_ref[...]).astype(BF16)
    for c in range(d // nc):
        cs = slice(c * nc, (c + 1) * nc)
        q_ref[:, cs] = _dot(h, wq_ref[:, cs])
        kt_ref[0, cs, :] = _dot_nt(wkt_ref[cs, :], h)
        vt_ref[0, cs, :] = _dot_nt(wvt_ref[cs, :], h)


def _qkv_t(x, g, wq, wkt, wvt, batch, seq, tm):
    rows, d = x.shape
    per_batch = seq // tm
    row_spec = pl.BlockSpec((tm, d), lambda i: (i, 0))
    t_spec = pl.BlockSpec((1, d, tm), lambda i: (i // per_batch, 0, i % per_batch))
    t_shape = jax.ShapeDtypeStruct((batch, d, seq), F32)
    return pl.pallas_call(
        functools.partial(_qkv_t_kernel, nc=512),
        out_shape=(jax.ShapeDtypeStruct((rows, d), F32), t_shape, t_shape),
        grid=(rows // tm,),
        in_specs=[row_spec, _resident((1, d)), _resident(wq.shape), _resident(wkt.shape),
                  _resident(wvt.shape)],
        out_specs=(row_spec, t_spec, t_spec),
        compiler_params=_params("parallel"),
        name="qkv_t",
    )(x, g, wq, wkt, wvt)


def _top3_bias(s, valid, lane):
    s = jnp.where(valid, s, -jnp.inf)
    bias = jnp.full(s.shape, NEG_BIG, F32)
    for _ in range(MOBA_TOPK):
        mx = jnp.max(s, axis=-1, keepdims=True)
        first = jnp.min(jnp.where(s == mx, lane, 1 << 20), axis=-1, keepdims=True)
        pick = lane == first
        bias = jnp.where(pick, 0.0, bias)
        s = jnp.where(pick, -jnp.inf, s)
    return jnp.where(valid, bias, NEG_BIG)


def _moba_kernel(q_ref, kt_ref, vt_ref, o_ref, kta_ref, vtb_ref, km_ref, *, nb):
    tq = q_ref.shape[0]
    blk = MOBA_BLOCK
    half = HEAD_DIM
    grp = KEY_GROUP * blk
    qi = pl.program_id(2)

    @pl.when(qi == 0)
    def _():
        sub_b = lax.broadcasted_iota(jnp.int32, (half, blk), 0)
        row = lax.broadcasted_iota(jnp.int32, (LANES, LANES), 0)
        col = lax.broadcasted_iota(jnp.int32, (LANES, LANES), 1)
        km = jnp.zeros((LANES, LANES), F32)
        for n in range(nb):
            cs = slice(n * blk, (n + 1) * blk)
            kb = kt_ref[0, :, cs]
            onehot = (sub_b == n).astype(BF16)
            kta_ref[0, :half, cs] = kb[:half].astype(BF16)
            kta_ref[0, half:, cs] = onehot
            kta_ref[1, :half, cs] = onehot
            kta_ref[1, half:, cs] = kb[half:].astype(BF16)
            vtb_ref[:, cs] = vt_ref[0, :, cs].astype(BF16)
            mean = jnp.mean(kb, axis=1, keepdims=True)
            km = jnp.where((col == n) | (col == half + n), mean, km)
        km_ref[...] = jnp.where((row < half) == (col >= half), km, 0.0)

    q2 = q_ref[...]
    lane = lax.broadcasted_iota(jnp.int32, (tq, LANES), 1)
    s_sel = _dot(q2, km_ref[...], precision=lax.Precision.HIGHEST)
    bias0 = _top3_bias(s_sel, (lane >= half) & (lane < half + qi), lane)
    bias1 = _top3_bias(s_sel, lane < qi, lane)

    qs = q2 * SCALE
    in0 = lane < half
    q_own = (jnp.where(in0, qs, 0.0).astype(BF16), jnp.where(in0, 0.0, qs).astype(BF16))
    q_past = (jnp.where(in0, qs, bias0).astype(BF16), jnp.where(in0, bias1, qs).astype(BF16))

    own = pl.ds(pl.multiple_of(qi * blk, blk), blk)
    causal = (lax.broadcasted_iota(jnp.int32, (tq, blk), 0) >= lax.broadcasted_iota(jnp.int32, (tq, blk), 1))
    v_own = vtb_ref[:, own]
    state = []
    for hh in range(2):
        s = jnp.where(causal, _dot(q_own[hh], kta_ref[hh, :, own]), NEG_BIG)
        m = jnp.max(s, axis=-1, keepdims=True)
        p = jnp.exp(s - m)
        state += [m, jnp.sum(p, axis=-1, keepdims=True), _dot_nt(p.astype(BF16), v_own)]

    def step(gi, carry):
        cols = pl.ds(pl.multiple_of(gi * grp, grp), grp)
        vg = vtb_ref[:, cols]
        out = []
        for hh in range(2):
            m, l, acc = carry[3 * hh: 3 * hh + 3]
            s = _dot(q_past[hh], kta_ref[hh, :, cols])
            m_new = jnp.maximum(m, jnp.max(s, axis=-1, keepdims=True))
            alpha = jnp.exp(m - m_new)
            p = jnp.exp(s - m_new)
            out += [m_new, alpha * l + jnp.sum(p, axis=-1, keepdims=True),
                    alpha * acc + _dot_nt(p.astype(BF16), vg)]
        return tuple(out)

    n_groups = (qi + KEY_GROUP - 1) // KEY_GROUP
    m0, l0, acc0, m1, l1, acc1 = lax.fori_loop(0, n_groups, step, tuple(state))
    o_ref[...] = jnp.where(in0, acc0 / l0, acc1 / l1).astype(o_ref.dtype)


def _moba_prompt(q, kt, vt):
    rows, d = q.shape
    batch, _, seq = kt.shape
    nb = seq // MOBA_BLOCK
    assert nb % KEY_GROUP == 0 and nb <= HEAD_DIM
    tq = MOBA_BLOCK
    kv_spec = pl.BlockSpec((1, LANES, seq), lambda b, hp, qi: (b, hp, 0))
    q_spec = pl.BlockSpec((tq, LANES), lambda b, hp, qi: (b * nb + qi, hp))
    return pl.pallas_call(
        functools.partial(_moba_kernel, nb=nb),
        out_shape=jax.ShapeDtypeStruct((rows, d), BF16),
        grid=(batch, d // LANES, nb),
        in_specs=[q_spec, kv_spec, kv_spec],
        out_specs=q_spec,
        scratch_shapes=[pltpu.VMEM((2, LANES, seq), BF16), pltpu.VMEM((LANES, seq), BF16),
                        pltpu.VMEM((LANES, LANES), F32)],
        compiler_params=_params("parallel", "parallel", "arbitrary"),
        name="moba_prompt",
    )(q, kt, vt)


def _decode_kernel(*refs, n_pages, pages_per_block):
    pt_ref, q_ref, qt_ref, kn_ref, vnt_ref = refs[:5]
    kp_refs = refs[5:5 + n_pages]
    vp_refs = refs[5 + n_pages:5 + 2 * n_pages]
    ot_ref, s_ref, p_ref = refs[5 + 2 * n_pages:]
    del pt_ref
    nh, hd = q_ref.shape[1], q_ref.shape[2]
    page = s_ref.shape[2]
    nbk = n_pages // pages_per_block
    qt = qt_ref[0]

    for h in range(nh):
        qcol = jnp.broadcast_to(qt[:, h:h + 1], (hd, page))
        for j in range(n_pages):
            s_ref[j, h:h + 1, :] = jnp.sum(kp_refs[j][0, 0, h] * qcol, axis=0, keepdims=True)

    s_all = s_ref[...]
    tot = jnp.sum(s_all, axis=-1, keepdims=True)
    sc = jnp.concatenate(
        [sum(tot[n * pages_per_block + i: n * pages_per_block + i + 1] for i in range(pages_per_block))
         for n in range(nbk)], axis=0)
    nidx = lax.broadcasted_iota(jnp.int32, sc.shape, 0)
    bias = jnp.full(sc.shape, NEG_BIG, F32)
    for _ in range(MOBA_TOPK):
        mx = jnp.max(sc, axis=0, keepdims=True)
        first = jnp.min(jnp.where(sc == mx, nidx, 1 << 20), axis=0, keepdims=True)
        pick = nidx == first
        bias = jnp.where(pick, 0.0, bias)
        sc = jnp.where(pick, -jnp.inf, sc)
    bias_p = jnp.concatenate([bias[j // pages_per_block: j // pages_per_block + 1]
                              for j in range(n_pages)], axis=0)

    s_own = jnp.sum(q_ref[0] * kn_ref[0], axis=-1, keepdims=True) * SCALE
    sm = s_all * SCALE + bias_p
    m = jnp.max(jnp.max(sm, axis=0), axis=-1, keepdims=True)
    m = jnp.maximum(m, s_own)
    p = jnp.exp(sm - m[None])
    e_own = jnp.exp(s_own - m)
    l = jnp.sum(jnp.sum(p, axis=0), axis=-1, keepdims=True) + e_own
    p_ref[...] = p
    w_own = e_own / l
    inv_l = 1.0 / l
    vnt = vnt_ref[0]
    for h in range(nh):
        acc = jnp.zeros((hd, page), F32)
        for j in range(n_pages):
            acc = acc + vp_refs[j][0, 0, h] * p_ref[j, h:h + 1, :]
        col = jnp.sum(acc, axis=1, keepdims=True)
        ot_ref[0, :, h:h + 1] = col * inv_l[h:h + 1, :] + vnt[:, h:h + 1] * w_own[h:h + 1, :]


def _moba_decode(q, k_new, v_new, cache_kt, cache_vt, page_table, layer):
    db, n_pages = page_table.shape
    _, _, nh, hd, page = cache_kt.shape
    ppb = MOBA_BLOCK // page
    tok = pl.BlockSpec((1, nh, hd), lambda b, pt: (b, 0, 0))
    tok_t = pl.BlockSpec((1, hd, nh), lambda b, pt: (b, 0, 0))
    pages = [pl.BlockSpec((1, 1, nh, hd, page), lambda b, pt, j=j: (layer, pt[b, j], 0, 0, 0))
             for j in range(n_pages)]
    grid_spec = pltpu.PrefetchScalarGridSpec(
        num_scalar_prefetch=1,
        grid=(db,),
        in_specs=[tok, tok_t, tok, tok_t] + pages + pages,
        out_specs=tok_t,
        scratch_shapes=[pltpu.VMEM((n_pages, nh, page), F32), pltpu.VMEM((n_pages, nh, page), F32)],
    )
    swap = lambda t: jnp.swapaxes(t, 1, 2)
    ot = pl.pallas_call(
        functools.partial(_decode_kernel, n_pages=n_pages, pages_per_block=ppb),
        out_shape=jax.ShapeDtypeStruct((db, hd, nh), F32),
        grid_spec=grid_spec,
        compiler_params=_params("parallel"),
        name="moba_decode",
    )(page_table, q, swap(q), k_new, swap(v_new), *([cache_kt] * n_pages), *([cache_vt] * n_pages))
    return swap(ot)


def _gmlp_kernel(y_ref, g_ref, wuv_ref, lng_ref, lnb_ref, ws_ref, bs_ref, wo_ref, o_ref, *rest,
                 single_pos, vc):
    if single_pos:
        vout_ref, vbuf = rest
    else:
        (vbuf,) = rest
    tm = y_ref.shape[0]
    half = vbuf.shape[1]
    gd = half // GMLP_GROUPS
    y = y_ref[...]
    h = _rms(y, g_ref[...]).astype(BF16)
    for c in range(half // vc):
        vbuf[:, c * vc:(c + 1) * vc] = jax.nn.gelu(_dot(h, wuv_ref[:, half + c * vc: half + (c + 1) * vc]))
    v = vbuf[...]
    xc = v - jnp.mean(v, axis=-1, keepdims=True)
    vn = xc * lax.rsqrt(jnp.mean(xc * xc, axis=-1, keepdims=True) + EPS) * lng_ref[...] + lnb_ref[...]
    vbuf[...] = vn
    if single_pos:
        vout_ref[...] = vn
    else:
        tril = (lax.broadcasted_iota(jnp.int32, (CHUNK, CHUNK), 0)
                >= lax.broadcasted_iota(jnp.int32, (CHUNK, CHUNK), 1))
    acc = y
    for g in range(GMLP_GROUPS):
        cols = slice(g * gd, (g + 1) * gd)
        u = jax.nn.gelu(_dot(h, wuv_ref[:, cols]))
        vg = vbuf[:, cols]
        if single_pos:
            f = ws_ref[g, 0:1, 0:1] * vg + bs_ref[g, 0:1, :]
        else:
            wsg = jnp.where(tril, ws_ref[g], 0.0).astype(BF16)
            bsg = bs_ref[g]
            f = jnp.concatenate(
                [_dot(wsg, vg[r * CHUNK:(r + 1) * CHUNK].astype(BF16)) + bsg for r in range(tm // CHUNK)],
                axis=0)
        acc = acc + _dot((u * f).astype(BF16), wo_ref[cols, :])
    o_ref[...] = acc


def _gmlp(y, g, wuv, lng, lnb, ws, bs, wo, tm, single_pos):
    rows, d = y.shape
    half = wo.shape[0]
    row_spec = pl.BlockSpec((tm, d), lambda i: (i, 0))
    v_spec = pl.BlockSpec((tm, half), lambda i: (i, 0))
    out_shape = jax.ShapeDtypeStruct((rows, d), F32)
    out_specs = row_spec
    if single_pos:
        out_shape = (out_shape, jax.ShapeDtypeStruct((rows, half), F32))
        out_specs = (row_spec, v_spec)
    return pl.pallas_call(
        functools.partial(_gmlp_kernel, single_pos=single_pos, vc=512),
        out_shape=out_shape,
        grid=(rows // tm,),
        in_specs=[row_spec, _resident((1, d)), _resident(wuv.shape), _resident((1, half)),
                  _resident((1, half)), _resident(ws.shape), _resident(bs.shape), _resident(wo.shape)],
        out_specs=out_specs,
        scratch_shapes=[pltpu.VMEM((tm, half), F32)],
        compiler_params=_params("parallel"),
        name="gmlp_single" if single_pos else "gmlp",
    )(y, g, wuv, lng, lnb, ws, bs, wo)


def _ffn_kernel(*refs, has_proj, has_final, fc):
    it = iter(refs)
    y_ref = next(it)
    y = y_ref[...]
    if has_proj:
        a_ref, wo_ref = next(it), next(it)
        y = y + _dot(a_ref[...], wo_ref[...])
    g_ref, wup_ref, wdn_ref = next(it), next(it), next(it)
    gf_ref = next(it) if has_final else None
    o_ref = next(it)
    h = _rms(y, g_ref[...]).astype(BF16)
    acc = y
    for c in range(wup_ref.shape[1] // fc):
        u = jnp.square(jnp.maximum(_dot(h, wup_ref[:, c * fc:(c + 1) * fc]), 0.0)).astype(BF16)
        acc = acc + _dot(u, wdn_ref[c * fc:(c + 1) * fc, :])
    if has_final:
        acc = _rms(acc, gf_ref[...])
    o_ref[...] = acc


def _ffn(y, g, wup, wdn, tm, proj=None, final_g=None):
    rows, d = y.shape
    row_spec = pl.BlockSpec((tm, d), lambda i: (i, 0))
    args, specs = [y], [row_spec]
    if proj is not None:
        a, wo = proj
        args += [a, wo]
        specs += [pl.BlockSpec((tm, a.shape[1]), lambda i: (i, 0)), _resident(wo.shape)]
    args += [g, wup, wdn]
    specs += [_resident((1, d)), _resident(wup.shape), _resident(wdn.shape)]
    if final_g is not None:
        args.append(final_g)
        specs.append(_resident((1, d)))
    return pl.pallas_call(
        functools.partial(_ffn_kernel, has_proj=proj is not None, has_final=final_g is not None, fc=512),
        out_shape=jax.ShapeDtypeStruct((rows, d), F32),
        grid=(rows // tm,),
        in_specs=specs,
        out_specs=row_spec,
        compiler_params=_params("parallel"),
        name="ffn",
    )(*args)


def kernel(x_prompt, x_sample, cache_k, cache_v, page_table, norm_mix, norm_ffn, w_qkv, w_o_attn,
           w_uv, ln_v_g, ln_v_b, w_s, b_s, w_o_gmlp, w_up, w_down, norm_final):
    batch, seq, d = x_prompt.shape
    db, ds, _ = x_sample.shape
    depth = norm_mix.shape[0]
    assert ds == 1 and seq % MOBA_BLOCK == 0 and d == N_HEADS * HEAD_DIM
    assert (page_table.shape[1] * PAGE_SIZE) % MOBA_BLOCK == 0 and cache_k.shape[2] == PAGE_SIZE
    tm_p, tm_s = 512, db

    yp = x_prompt.reshape(batch * seq, d)
    ys = x_sample.reshape(db * ds, d)
    w_qkv_b, w_o_attn_b, w_uv_b = w_qkv.astype(BF16), w_o_attn.astype(BF16), w_uv.astype(BF16)
    w_o_gmlp_b, w_up_b, w_down_b = w_o_gmlp.astype(BF16), w_up.astype(BF16), w_down.astype(BF16)
    cache_kt = jnp.transpose(cache_k, (0, 1, 3, 4, 2))
    cache_vt = jnp.transpose(cache_v, (0, 1, 3, 4, 2))
    row = lambda t: t.reshape(1, -1)
    heads = lambda t: t.reshape(db, N_HEADS, HEAD_DIM)
    untranspose = lambda t: jnp.transpose(t.reshape(batch, N_HEADS, HEAD_DIM, seq), (0, 3, 1, 2))

    k_p, v_p, k_s, v_s, gv_s = [], [], [], [], []
    for i in range(depth):
        g_mix, g_ffn = row(norm_mix[i]), row(norm_ffn[i])
        final_g = row(norm_final) if i == depth - 1 else None
        if i % 2 == 0:
            a = i // 2
            wq, wkt, wvt = w_qkv_b[a, :, :d], w_qkv_b[a, :, d:2 * d].T, w_qkv_b[a, :, 2 * d:].T
            qp, ktp, vtp = _qkv_t(yp, g_mix, wq, wkt, wvt, batch, seq, tm_p)
            qs, kn, vn = _qkv(ys, g_mix, w_qkv_b[a], tm_s)
            op = _moba_prompt(qp, ktp, vtp)
            os_ = _moba_decode(heads(qs), heads(kn), heads(vn), cache_kt, cache_vt, page_table, a)
            k_p.append(untranspose(ktp))
            v_p.append(untranspose(vtp))
            k_s.append(kn.reshape(db, ds, N_HEADS, HEAD_DIM))
            v_s.append(vn.reshape(db, ds, N_HEADS, HEAD_DIM))
            yp = _ffn(yp, g_ffn, w_up_b[i], w_down_b[i], tm_p, proj=(op, w_o_attn_b[a]), final_g=final_g)
            ys = _ffn(ys, g_ffn, w_up_b[i], w_down_b[i], tm_s,
                      proj=(os_.reshape(db, d).astype(BF16), w_o_attn_b[a]), final_g=final_g)
        else:
            gi = i // 2
            mix = (w_uv_b[gi], row(ln_v_g[gi]), row(ln_v_b[gi]), w_s[gi], b_s[gi][:, :, None], w_o_gmlp_b[gi])
            yp = _gmlp(yp, g_mix, *mix, tm=256, single_pos=False)
            ys, vrows = _gmlp(ys, g_mix, *mix, tm=tm_s, single_pos=True)
            gv_s.append(vrows.reshape(db, ds, -1))
            yp = _ffn(yp, g_ffn, w_up_b[i], w_down_b[i], tm_p, final_g=final_g)
            ys = _ffn(ys, g_ffn, w_up_b[i], w_down_b[i], tm_s, final_g=final_g)
    return (yp.reshape(batch, seq, d), ys.reshape(db, ds, d), jnp.stack(k_p), jnp.stack(v_p),
            jnp.stack(k_s), jnp.stack(v_s), jnp.stack(gv_s))
```

```python
import functools

import jax
import jax.numpy as jnp
from jax import lax
from jax.experimental import pallas as pl
from jax.experimental.pallas import tpu as pltpu

F32 = jnp.float32
BF16 = jnp.bfloat16

N_HEADS = 16
HEAD_DIM = 64
MOBA_BLOCK = 256
MOBA_TOPK = 3
CHUNK = 128
GMLP_GROUPS = 8
PAGE_SIZE = 128
EPS = 1e-6
NEG_BIG = -1e30
LANES = 128
SCALE = HEAD_DIM ** -0.5
LOG2E = 1.4426950408889634
KEY_GROUP = 4

VMEM_LIMIT = 56 * 1024 * 1024


def _params(*sem):
    return pltpu.CompilerParams(dimension_semantics=sem, vmem_limit_bytes=VMEM_LIMIT)


def _resident(shape):
    nd = len(shape)
    return pl.BlockSpec(shape, lambda *_: (0,) * nd, pipeline_mode=pl.Buffered(1))


def _rms(x, g):
    return x * lax.rsqrt(jnp.mean(x * x, axis=-1, keepdims=True) + EPS) * g


def _dot(a, b, precision=None):
    return jnp.dot(a, b, preferred_element_type=F32, precision=precision)


def _dot_nt(a, b):
    return lax.dot_general(a, b, (((1,), (1,)), ((), ())), preferred_element_type=F32)


def _qkv_kernel(x_ref, g_ref, w_ref, q_ref, k_ref, v_ref, *, nc):
    d = x_ref.shape[1]
    h = _rms(x_ref[...], g_ref[...]).astype(BF16)
    for j, o_ref in enumerate((q_ref, k_ref, v_ref)):
        for c in range(d // nc):
            o_ref[:, c * nc:(c + 1) * nc] = _dot(h, w_ref[:, j * d + c * nc: j * d + (c + 1) * nc])


def _qkv(x, g, w, tm):
    rows, d = x.shape
    out = jax.ShapeDtypeStruct((rows, d), F32)
    row_spec = pl.BlockSpec((tm, d), lambda i: (i, 0))
    return pl.pallas_call(
        functools.partial(_qkv_kernel, nc=512),
        out_shape=(out, out, out),
        grid=(rows // tm,),
        in_specs=[row_spec, _resident((1, d)), _resident(w.shape)],
        out_specs=(row_spec, row_spec, row_spec),
        compiler_params=_params("parallel"),
        name="qkv",
    )(x, g, w)


def _qkv_t_kernel(x_ref, g_ref, wt_ref, wk_ref, *rest, nc):
    qt_ref, kt_ref, vt_ref, k_ref = rest[-4:]
    d = x_ref.shape[1]
    h = _rms(x_ref[...], g_ref[...]).astype(BF16)
    for c in range(d // nc):
        cs = slice(c * nc, (c + 1) * nc)
        qt_ref[0, cs, :] = _dot_nt(wt_ref[cs, :], h)
        kt_ref[0, 0, cs, :] = _dot_nt(wt_ref[d + c * nc: d + (c + 1) * nc, :], h)
        vt_ref[0, 0, cs, :] = _dot_nt(wt_ref[2 * d + c * nc: 2 * d + (c + 1) * nc, :], h)
        k_ref[:, cs] = _dot(h, wk_ref[:, cs])


def _qkv_t(x, g, wt, wk, kv_prev, layer, n_layers, batch, seq, tm):
    rows, d = x.shape
    per_batch = seq // tm
    row_spec = pl.BlockSpec((tm, d), lambda i: (i, 0))
    qt_spec = pl.BlockSpec((1, d, tm), lambda i: (i // per_batch, 0, i % per_batch))
    kv_spec = pl.BlockSpec((1, 1, d, tm), lambda i: (layer, i // per_batch, 0, i % per_batch))
    kv_shape = jax.ShapeDtypeStruct((n_layers, batch, d, seq), F32)
    args = [x, g, wt, wk]
    specs = [row_spec, _resident((1, d)), _resident(wt.shape), _resident(wk.shape)]
    aliases = {}
    if kv_prev is not None:
        args += list(kv_prev)
        specs += [pl.BlockSpec(memory_space=pl.ANY)] * 2
        aliases = {4: 1, 5: 2}
    return pl.pallas_call(
        functools.partial(_qkv_t_kernel, nc=512),
        out_shape=(jax.ShapeDtypeStruct((batch, d, seq), F32), kv_shape, kv_shape,
                   jax.ShapeDtypeStruct((rows, d), F32)),
        grid=(rows // tm,),
        in_specs=specs,
        out_specs=(qt_spec, kv_spec, kv_spec, row_spec),
        input_output_aliases=aliases,
        compiler_params=_params("parallel"),
        name="qkv_t",
    )(*args)


def _top3_bias(s, n_valid):
    ridx = lax.broadcasted_iota(jnp.int32, s.shape, 0)
    valid = ridx < n_valid
    s = jnp.where(valid, s, -jnp.inf)
    bias = jnp.full(s.shape, NEG_BIG, F32)
    for _ in range(MOBA_TOPK):
        mx = jnp.max(s, axis=0, keepdims=True)
        first = jnp.min(jnp.where(s == mx, ridx, 1 << 20), axis=0, keepdims=True)
        pick = ridx == first
        bias = jnp.where(pick, 0.0, bias)
        s = jnp.where(pick, -jnp.inf, s)
    return jnp.where(valid, bias, NEG_BIG)


def _moba_kernel(qt_ref, k_ref, vt_ref, o_ref, ka_ref, vtb_ref, kmt_ref, s_ref, p_ref, *, nb):
    tq = qt_ref.shape[2]
    blk = MOBA_BLOCK
    half = HEAD_DIM
    grp = KEY_GROUP * blk
    qi = pl.program_id(2)

    @pl.when(qi == 0)
    def _():
        kmt_ref[...] = jnp.zeros(kmt_ref.shape, F32)
        lane_b = lax.broadcasted_iota(jnp.int32, (blk, LANES), 1)
        lane1 = lax.broadcasted_iota(jnp.int32, (1, LANES), 1)

        def build(n, carry):
            rows = pl.ds(pl.multiple_of(n * blk, blk), blk)
            kb = k_ref[rows, :]
            ka_ref[0, rows, :] = jnp.where(lane_b < half, kb, (lane_b - half == n).astype(F32)).astype(BF16)
            ka_ref[1, rows, :] = jnp.where(lane_b >= half, kb, (lane_b == n).astype(F32)).astype(BF16)
            km = jnp.mean(kb, axis=0, keepdims=True)
            kmt_ref[pl.ds(n, 1), :] = jnp.where(lane1 >= half, km, 0.0)
            kmt_ref[pl.ds(half + n, 1), :] = jnp.where(lane1 < half, km, 0.0)
            return carry

        lax.fori_loop(0, nb, build, 0)
        for n in range(nb):
            vtb_ref[:, n * blk:(n + 1) * blk] = vt_ref[0, 0, :, n * blk:(n + 1) * blk].astype(BF16)

    qt = qt_ref[0]
    s_sel = _dot(kmt_ref[...], qt, precision=lax.Precision.HIGHEST)
    sel_rows = 8 * pl.cdiv(nb, 8)
    bias1 = _top3_bias(s_sel[:sel_rows], qi)
    bias0 = _top3_bias(s_sel[half:half + sel_rows], qi)
    qs = qt * (SCALE * LOG2E)
    pad = jnp.zeros((half - sel_rows, tq), F32)
    zeros = jnp.zeros((half, tq), F32)
    q_past = (jnp.concatenate([qs[:half], bias0, pad], axis=0).astype(BF16),
              jnp.concatenate([bias1, pad, qs[half:]], axis=0).astype(BF16))
    q_own = (jnp.concatenate([qs[:half], zeros], axis=0).astype(BF16),
             jnp.concatenate([zeros, qs[half:]], axis=0).astype(BF16))

    own = pl.ds(pl.multiple_of(qi * blk, blk), blk)
    causal = (lax.broadcasted_iota(jnp.int32, (blk, tq), 0) <= lax.broadcasted_iota(jnp.int32, (blk, tq), 1))
    state = []
    for hh in range(2):
        s = jnp.where(causal, _dot(ka_ref[hh, own, :], q_own[hh]), NEG_BIG)
        m = jnp.max(s, axis=0, keepdims=True)
        p = jnp.exp2(s - m)
        state += [m, jnp.sum(p, axis=0, keepdims=True),
                  _dot(vtb_ref[hh * half:(hh + 1) * half, own], p.astype(BF16))]

    def rows_of(gi):
        return pl.ds(pl.multiple_of(gi * grp, grp), grp)

    def qk(gi, slot):
        for hh in range(2):
            s_ref[slot, hh] = _dot(ka_ref[hh, rows_of(gi), :], q_past[hh])

    def softmax_pv(gi, slot, carry):
        out = []
        for hh in range(2):
            m, l, acc = carry[3 * hh: 3 * hh + 3]
            s = s_ref[slot, hh]
            m_new = jnp.maximum(m, jnp.max(s, axis=0, keepdims=True))
            alpha = jnp.exp2(m - m_new)
            p = jnp.exp2(s - m_new)
            p_ref[slot, hh] = p.astype(BF16)
            out += [m_new, alpha * l + jnp.sum(p, axis=0, keepdims=True),
                    alpha * acc + _dot(vtb_ref[hh * half:(hh + 1) * half, rows_of(gi)], p_ref[slot, hh])]
        return tuple(out)

    last_group = nb // KEY_GROUP - 1

    def step(pi, carry):
        g0 = 2 * pi
        qk(g0 + 1, 1)
        carry = softmax_pv(g0, 0, carry)
        qk(jnp.minimum(g0 + 2, last_group), 0)
        return softmax_pv(g0 + 1, 1, carry)

    qk(0, 0)
    n_pairs = (qi + 2 * KEY_GROUP - 1) // (2 * KEY_GROUP)
    m0, l0, acc0, m1, l1, acc1 = lax.fori_loop(0, n_pairs, step, tuple(state))
    out_t = jnp.concatenate([acc0 / l0, acc1 / l1], axis=0).astype(BF16)
    eye = (lax.broadcasted_iota(jnp.int32, (tq, tq), 0)
           == lax.broadcasted_iota(jnp.int32, (tq, tq), 1)).astype(BF16)
    o_ref[...] = _dot_nt(eye, out_t).astype(o_ref.dtype)


def _moba_prompt(qt, k, vt_all, layer):
    batch, d, seq = qt.shape
    nb = seq // MOBA_BLOCK
    assert nb % (2 * KEY_GROUP) == 0 and 8 * pl.cdiv(nb, 8) <= HEAD_DIM
    tq = MOBA_BLOCK
    return pl.pallas_call(
        functools.partial(_moba_kernel, nb=nb),
        out_shape=jax.ShapeDtypeStruct((batch * seq, d), BF16),
        grid=(batch, d // LANES, nb),
        in_specs=[pl.BlockSpec((1, LANES, tq), lambda b, hp, qi: (b, hp, qi)),
                  pl.BlockSpec((seq, LANES), lambda b, hp, qi: (b, hp)),
                  pl.BlockSpec((1, 1, LANES, seq), lambda b, hp, qi: (layer, b, hp, 0))],
        out_specs=pl.BlockSpec((tq, LANES), lambda b, hp, qi: (b * nb + qi, hp)),
        scratch_shapes=[pltpu.VMEM((2, seq, LANES), BF16), pltpu.VMEM((LANES, seq), BF16),
                        pltpu.VMEM((LANES, LANES), F32),
                        pltpu.VMEM((2, 2, KEY_GROUP * MOBA_BLOCK, tq), F32),
                        pltpu.VMEM((2, 2, KEY_GROUP * MOBA_BLOCK, tq), BF16)],
        compiler_params=_params("parallel", "parallel", "arbitrary"),
        name="moba_prompt",
    )(qt, k, vt_all)


def _decode_kernel(*refs, n_pages, pages_per_block):
    pt_ref, q_ref, qt_ref, kn_ref, vnt_ref = refs[:5]
    kp_refs = refs[5:5 + n_pages]
    vp_refs = refs[5 + n_pages:5 + 2 * n_pages]
    ot_ref, s_ref, p_ref = refs[5 + 2 * n_pages:]
    del pt_ref
    nh, hd = q_ref.shape[1], q_ref.shape[2]
    page = s_ref.shape[2]
    nbk = n_pages // pages_per_block
    qt = qt_ref[0]

    for h in range(nh):
        qcol = jnp.broadcast_to(qt[:, h:h + 1], (hd, page))
        for j in range(n_pages):
            s_ref[j, h:h + 1, :] = jnp.sum(kp_refs[j][0, 0, h] * qcol, axis=0, keepdims=True)

    s_all = s_ref[...]
    tot = jnp.sum(s_all, axis=-1, keepdims=True)
    sc = jnp.concatenate(
        [sum(tot[n * pages_per_block + i: n * pages_per_block + i + 1] for i in range(pages_per_block))
         for n in range(nbk)], axis=0)
    nidx = lax.broadcasted_iota(jnp.int32, sc.shape, 0)
    bias = jnp.full(sc.shape, NEG_BIG, F32)
    for _ in range(MOBA_TOPK):
        mx = jnp.max(sc, axis=0, keepdims=True)
        first = jnp.min(jnp.where(sc == mx, nidx, 1 << 20), axis=0, keepdims=True)
        pick = nidx == first
        bias = jnp.where(pick, 0.0, bias)
        sc = jnp.where(pick, -jnp.inf, sc)
    bias_p = jnp.concatenate([bias[j // pages_per_block: j // pages_per_block + 1]
                              for j in range(n_pages)], axis=0)

    s_own = jnp.sum(q_ref[0] * kn_ref[0], axis=-1, keepdims=True) * SCALE
    sm = s_all * SCALE + bias_p
    m = jnp.max(jnp.max(sm, axis=0), axis=-1, keepdims=True)
    m = jnp.maximum(m, s_own)
    p = jnp.exp(sm - m[None])
    e_own = jnp.exp(s_own - m)
    l = jnp.sum(jnp.sum(p, axis=0), axis=-1, keepdims=True) + e_own
    p_ref[...] = p
    w_own = e_own / l
    inv_l = 1.0 / l
    vnt = vnt_ref[0]
    for h in range(nh):
        acc = jnp.zeros((hd, page), F32)
        for j in range(n_pages):
            acc = acc + vp_refs[j][0, 0, h] * p_ref[j, h:h + 1, :]
        col = jnp.sum(acc, axis=1, keepdims=True)
        ot_ref[0, :, h:h + 1] = col * inv_l[h:h + 1, :] + vnt[:, h:h + 1] * w_own[h:h + 1, :]


def _moba_decode(q, k_new, v_new, cache_kt, cache_vt, page_table, layer):
    db, n_pages = page_table.shape
    _, _, nh, hd, page = cache_kt.shape
    ppb = MOBA_BLOCK // page
    tok = pl.BlockSpec((1, nh, hd), lambda b, pt: (b, 0, 0))
    tok_t = pl.BlockSpec((1, hd, nh), lambda b, pt: (b, 0, 0))
    pages = [pl.BlockSpec((1, 1, nh, hd, page), lambda b, pt, j=j: (layer, pt[b, j], 0, 0, 0))
             for j in range(n_pages)]
    grid_spec = pltpu.PrefetchScalarGridSpec(
        num_scalar_prefetch=1,
        grid=(db,),
        in_specs=[tok, tok_t, tok, tok_t] + pages + pages,
        out_specs=tok_t,
        scratch_shapes=[pltpu.VMEM((n_pages, nh, page), F32), pltpu.VMEM((n_pages, nh, page), F32)],
    )
    swap = lambda t: jnp.swapaxes(t, 1, 2)
    ot = pl.pallas_call(
        functools.partial(_decode_kernel, n_pages=n_pages, pages_per_block=ppb),
        out_shape=jax.ShapeDtypeStruct((db, hd, nh), F32),
        grid_spec=grid_spec,
        compiler_params=_params("parallel"),
        name="moba_decode",
    )(page_table, q, swap(q), k_new, swap(v_new), *([cache_kt] * n_pages), *([cache_vt] * n_pages))
    return swap(ot)


def _gmlp_kernel(y_ref, g_ref, wuv_ref, lng_ref, lnb_ref, ws_ref, bs_ref, wo_ref, o_ref, *rest,
                 single_pos, vc):
    if single_pos:
        vout_ref, vbuf = rest
    else:
        (vbuf,) = rest
    tm = y_ref.shape[0]
    half = vbuf.shape[1]
    gd = half // GMLP_GROUPS
    y = y_ref[...]
    h = _rms(y, g_ref[...]).astype(BF16)
    for c in range(half // vc):
        vbuf[:, c * vc:(c + 1) * vc] = jax.nn.gelu(_dot(h, wuv_ref[:, half + c * vc: half + (c + 1) * vc]))
    v = vbuf[...]
    xc = v - jnp.mean(v, axis=-1, keepdims=True)
    vn = xc * lax.rsqrt(jnp.mean(xc * xc, axis=-1, keepdims=True) + EPS) * lng_ref[...] + lnb_ref[...]
    vbuf[...] = vn
    if single_pos:
        vout_ref[...] = vn
    else:
        tril = (lax.broadcasted_iota(jnp.int32, (CHUNK, CHUNK), 0)
                >= lax.broadcasted_iota(jnp.int32, (CHUNK, CHUNK), 1))
    for g in range(GMLP_GROUPS):
        cols = slice(g * gd, (g + 1) * gd)
        vg = vbuf[:, cols]
        if single_pos:
            vbuf[:, cols] = ws_ref[g, 0:1, 0:1] * vg + bs_ref[g, 0:1, :]
        else:
            wsg = jnp.where(tril, ws_ref[g], 0.0).astype(BF16)
            bsg = bs_ref[g]
            for r in range(tm // CHUNK):
                rs = slice(r * CHUNK, (r + 1) * CHUNK)
                vbuf[rs, cols] = _dot(wsg, vg[rs].astype(BF16)) + bsg
    acc = y
    for c in range(half // vc):
        cs = slice(c * vc, (c + 1) * vc)
        u = jax.nn.gelu(_dot(h, wuv_ref[:, cs]))
        acc = acc + _dot((u * vbuf[:, cs]).astype(BF16), wo_ref[cs, :])
    o_ref[...] = acc


def _gmlp(y, g, wuv, lng, lnb, ws, bs, wo, tm, single_pos):
    rows, d = y.shape
    half = wo.shape[0]
    row_spec = pl.BlockSpec((tm, d), lambda i: (i, 0))
    v_spec = pl.BlockSpec((tm, half), lambda i: (i, 0))
    out_shape = jax.ShapeDtypeStruct((rows, d), F32)
    out_specs = row_spec
    if single_pos:
        out_shape = (out_shape, jax.ShapeDtypeStruct((rows, half), F32))
        out_specs = (row_spec, v_spec)
    return pl.pallas_call(
        functools.partial(_gmlp_kernel, single_pos=single_pos, vc=512),
        out_shape=out_shape,
        grid=(rows // tm,),
        in_specs=[row_spec, _resident((1, d)), _resident(wuv.shape), _resident((1, half)),
                  _resident((1, half)), _resident(ws.shape), _resident(bs.shape), _resident(wo.shape)],
        out_specs=out_specs,
        scratch_shapes=[pltpu.VMEM((tm, half), F32)],
        compiler_params=_params("parallel"),
        name="gmlp_single" if single_pos else "gmlp",
    )(y, g, wuv, lng, lnb, ws, bs, wo)


def _ffn_kernel(*refs, has_proj, has_final, fc):
    it = iter(refs)
    y_ref = next(it)
    y = y_ref[...]
    if has_proj:
        a_ref, wo_ref = next(it), next(it)
        y = y + _dot(a_ref[...], wo_ref[...])
    g_ref, wup_ref, wdn_ref = next(it), next(it), next(it)
    gf_ref = next(it) if has_final else None
    o_ref = next(it)
    h = _rms(y, g_ref[...]).astype(BF16)
    acc = y
    for c in range(wup_ref.shape[1] // fc):
        u = jnp.square(jnp.maximum(_dot(h, wup_ref[:, c * fc:(c + 1) * fc]), 0.0)).astype(BF16)
        acc = acc + _dot(u, wdn_ref[c * fc:(c + 1) * fc, :])
    if has_final:
        acc = _rms(acc, gf_ref[...])
    o_ref[...] = acc


def _ffn(y, g, wup, wdn, tm, proj=None, final_g=None):
    rows, d = y.shape
    row_spec = pl.BlockSpec((tm, d), lambda i: (i, 0))
    args, specs = [y], [row_spec]
    if proj is not None:
        a, wo = proj
        args += [a, wo]
        specs += [pl.BlockSpec((tm, a.shape[1]), lambda i: (i, 0)), _resident(wo.shape)]
    args += [g, wup, wdn]
    specs += [_resident((1, d)), _resident(wup.shape), _resident(wdn.shape)]
    if final_g is not None:
        args.append(final_g)
        specs.append(_resident((1, d)))
    return pl.pallas_call(
        functools.partial(_ffn_kernel, has_proj=proj is not None, has_final=final_g is not None, fc=512),
        out_shape=jax.ShapeDtypeStruct((rows, d), F32),
        grid=(rows // tm,),
        in_specs=specs,
        out_specs=row_spec,
        compiler_params=_params("parallel"),
        name="ffn",
    )(*args)


def kernel(x_prompt, x_sample, cache_k, cache_v, page_table, norm_mix, norm_ffn, w_qkv, w_o_attn,
           w_uv, ln_v_g, ln_v_b, w_s, b_s, w_o_gmlp, w_up, w_down, norm_final):
    batch, seq, d = x_prompt.shape
    db, ds, _ = x_sample.shape
    depth = norm_mix.shape[0]
    assert ds == 1 and seq % MOBA_BLOCK == 0 and d == N_HEADS * HEAD_DIM
    assert (page_table.shape[1] * PAGE_SIZE) % MOBA_BLOCK == 0 and cache_k.shape[2] == PAGE_SIZE
    tm_p, tm_s = 512, db

    yp = x_prompt.reshape(batch * seq, d)
    ys = x_sample.reshape(db * ds, d)
    w_qkv_b, w_o_attn_b, w_uv_b = w_qkv.astype(BF16), w_o_attn.astype(BF16), w_uv.astype(BF16)
    w_o_gmlp_b, w_up_b, w_down_b = w_o_gmlp.astype(BF16), w_up.astype(BF16), w_down.astype(BF16)
    w_qkv_t = jnp.swapaxes(w_qkv_b, 1, 2)
    n_attn = w_qkv.shape[0]
    cache_kt = jnp.transpose(cache_k, (0, 1, 3, 4, 2))
    cache_vt = jnp.transpose(cache_v, (0, 1, 3, 4, 2))
    row = lambda t: t.reshape(1, -1)
    heads = lambda t: t.reshape(db, N_HEADS, HEAD_DIM)

    kv_t = None
    k_s, v_s, gv_s = [], [], []
    for i in range(depth):
        g_mix, g_ffn = row(norm_mix[i]), row(norm_ffn[i])
        final_g = row(norm_final) if i == depth - 1 else None
        if i % 2 == 0:
            a = i // 2
            qtp, ktp, vtp, kp = _qkv_t(yp, g_mix, w_qkv_t[a], w_qkv_b[a, :, d:2 * d], kv_t, a, n_attn,
                                       batch, seq, tm_p)
            kv_t = (ktp, vtp)
            qs, kn, vn = _qkv(ys, g_mix, w_qkv_b[a], tm_s)
            op = _moba_prompt(qtp, kp, vtp, a)
            os_ = _moba_decode(heads(qs), heads(kn), heads(vn), cache_kt, cache_vt, page_table, a)
            k_s.append(kn.reshape(db, ds, N_HEADS, HEAD_DIM))
            v_s.append(vn.reshape(db, ds, N_HEADS, HEAD_DIM))
            yp = _ffn(yp, g_ffn, w_up_b[i], w_down_b[i], tm_p, proj=(op, w_o_attn_b[a]), final_g=final_g)
            ys = _ffn(ys, g_ffn, w_up_b[i], w_down_b[i], tm_s,
                      proj=(os_.reshape(db, d).astype(BF16), w_o_attn_b[a]), final_g=final_g)
        else:
            gi = i // 2
            mix = (w_uv_b[gi], row(ln_v_g[gi]), row(ln_v_b[gi]), w_s[gi], b_s[gi][:, :, None], w_o_gmlp_b[gi])
            yp = _gmlp(yp, g_mix, *mix, tm=tm_p, single_pos=False)
            ys, vrows = _gmlp(ys, g_mix, *mix, tm=tm_s, single_pos=True)
            gv_s.append(vrows.reshape(db, ds, -1))
            yp = _ffn(yp, g_ffn, w_up_b[i], w_down_b[i], tm_p, final_g=final_g)
            ys = _ffn(ys, g_ffn, w_up_b[i], w_down_b[i], tm_s, final_g=final_g)
    untranspose = lambda t: jnp.transpose(t.reshape(n_attn, batch, N_HEADS, HEAD_DIM, seq), (0, 1, 4, 2, 3))
    return (yp.reshape(batch, seq, d), ys.reshape(db, ds, d), untranspose(kv_t[0]), untranspose(kv_t[1]),
            jnp.stack(k_s), jnp.stack(v_s), jnp.stack(gv_s))
```

```python
import functools

import jax
import jax.numpy as jnp
from jax import lax
from jax.experimental import pallas as pl
from jax.experimental.pallas import tpu as pltpu

F32 = jnp.float32
BF16 = jnp.bfloat16

N_HEADS = 16
HEAD_DIM = 64
MOBA_BLOCK = 256
MOBA_TOPK = 3
CHUNK = 128
GMLP_GROUPS = 8
PAGE_SIZE = 128
EPS = 1e-6
NEG_BIG = -1e30
LANES = 128
SCALE = HEAD_DIM ** -0.5
LOG2E = 1.4426950408889634
KEY_GROUP = 2

VMEM_LIMIT = 56 * 1024 * 1024


def _params(*sem):
    return pltpu.CompilerParams(dimension_semantics=sem, vmem_limit_bytes=VMEM_LIMIT)


def _resident(shape):
    nd = len(shape)
    return pl.BlockSpec(shape, lambda *_: (0,) * nd, pipeline_mode=pl.Buffered(1))


def _rms(x, g):
    return x * lax.rsqrt(jnp.mean(x * x, axis=-1, keepdims=True) + EPS) * g


def _dot(a, b, precision=None):
    return jnp.dot(a, b, preferred_element_type=F32, precision=precision)


def _dot_nt(a, b):
    return lax.dot_general(a, b, (((1,), (1,)), ((), ())), preferred_element_type=F32)


def _qkv_kernel(x_ref, g_ref, w_ref, q_ref, k_ref, v_ref, *, nc):
    d = x_ref.shape[1]
    h = _rms(x_ref[...], g_ref[...]).astype(BF16)
    for j, o_ref in enumerate((q_ref, k_ref, v_ref)):
        for c in range(d // nc):
            o_ref[:, c * nc:(c + 1) * nc] = _dot(h, w_ref[:, j * d + c * nc: j * d + (c + 1) * nc])


def _qkv(x, g, w, tm):
    rows, d = x.shape
    out = jax.ShapeDtypeStruct((rows, d), F32)
    row_spec = pl.BlockSpec((tm, d), lambda i: (i, 0))
    return pl.pallas_call(
        functools.partial(_qkv_kernel, nc=512),
        out_shape=(out, out, out),
        grid=(rows // tm,),
        in_specs=[row_spec, _resident((1, d)), _resident(w.shape)],
        out_specs=(row_spec, row_spec, row_spec),
        compiler_params=_params("parallel"),
        name="qkv",
    )(x, g, w)


def _qkv_t_kernel(x_ref, g_ref, wt_ref, wk_ref, *rest, nc):
    qt_ref, kt_ref, vt_ref, k_ref = rest[-4:]
    d = x_ref.shape[1]
    h = _rms(x_ref[...], g_ref[...]).astype(BF16)
    for c in range(d // nc):
        cs = slice(c * nc, (c + 1) * nc)
        qt_ref[0, cs, :] = _dot_nt(wt_ref[cs, :], h)
        kt_ref[0, 0, cs, :] = _dot_nt(wt_ref[d + c * nc: d + (c + 1) * nc, :], h)
        vt_ref[0, 0, cs, :] = _dot_nt(wt_ref[2 * d + c * nc: 2 * d + (c + 1) * nc, :], h)
        k_ref[:, cs] = _dot(h, wk_ref[:, cs])


def _qkv_t(x, g, wt, wk, kv_prev, layer, n_layers, batch, seq, tm):
    rows, d = x.shape
    per_batch = seq // tm
    row_spec = pl.BlockSpec((tm, d), lambda i: (i, 0))
    qt_spec = pl.BlockSpec((1, d, tm), lambda i: (i // per_batch, 0, i % per_batch))
    kv_spec = pl.BlockSpec((1, 1, d, tm), lambda i: (layer, i // per_batch, 0, i % per_batch))
    kv_shape = jax.ShapeDtypeStruct((n_layers, batch, d, seq), F32)
    args = [x, g, wt, wk]
    specs = [row_spec, _resident((1, d)), _resident(wt.shape), _resident(wk.shape)]
    aliases = {}
    if kv_prev is not None:
        args += list(kv_prev)
        specs += [pl.BlockSpec(memory_space=pl.ANY)] * 2
        aliases = {4: 1, 5: 2}
    return pl.pallas_call(
        functools.partial(_qkv_t_kernel, nc=512),
        out_shape=(jax.ShapeDtypeStruct((batch, d, seq), F32), kv_shape, kv_shape,
                   jax.ShapeDtypeStruct((rows, d), F32)),
        grid=(rows // tm,),
        in_specs=specs,
        out_specs=(qt_spec, kv_spec, kv_spec, row_spec),
        input_output_aliases=aliases,
        compiler_params=_params("parallel"),
        name="qkv_t",
    )(*args)


def _top3_bias(s, n_valid):
    ridx = lax.broadcasted_iota(jnp.int32, s.shape, 0)
    valid = ridx < n_valid
    s = jnp.where(valid, s, -jnp.inf)
    bias = jnp.full(s.shape, NEG_BIG, F32)
    for _ in range(MOBA_TOPK):
        mx = jnp.max(s, axis=0, keepdims=True)
        first = jnp.min(jnp.where(s == mx, ridx, 1 << 20), axis=0, keepdims=True)
        pick = ridx == first
        bias = jnp.where(pick, 0.0, bias)
        s = jnp.where(pick, -jnp.inf, s)
    return jnp.where(valid, bias, NEG_BIG)


def _moba_kernel(qt_ref, k_ref, vt_ref, o_ref, ka_ref, vtb_ref, kmt_ref, s_ref, p_ref, *, nb):
    tq = qt_ref.shape[2]
    blk = MOBA_BLOCK
    half = HEAD_DIM
    grp = KEY_GROUP * blk
    qi = pl.program_id(2)

    @pl.when(qi == 0)
    def _():
        kmt_ref[...] = jnp.zeros(kmt_ref.shape, F32)
        lane_b = lax.broadcasted_iota(jnp.int32, (blk, LANES), 1)
        lane1 = lax.broadcasted_iota(jnp.int32, (1, LANES), 1)

        def build(n, carry):
            rows = pl.ds(pl.multiple_of(n * blk, blk), blk)
            kb = k_ref[rows, :]
            ka_ref[0, rows, :] = jnp.where(lane_b < half, kb, (lane_b - half == n).astype(F32)).astype(BF16)
            ka_ref[1, rows, :] = jnp.where(lane_b >= half, kb, (lane_b == n).astype(F32)).astype(BF16)
            km = jnp.mean(kb, axis=0, keepdims=True)
            kmt_ref[pl.ds(n, 1), :] = jnp.where(lane1 >= half, km, 0.0)
            kmt_ref[pl.ds(half + n, 1), :] = jnp.where(lane1 < half, km, 0.0)
            return carry

        lax.fori_loop(0, nb, build, 0)
        for n in range(nb):
            vtb_ref[:, n * blk:(n + 1) * blk] = vt_ref[0, 0, :, n * blk:(n + 1) * blk].astype(BF16)

    qt = qt_ref[0]
    s_sel = _dot(kmt_ref[...], qt, precision=lax.Precision.HIGHEST)
    sel_rows = 8 * pl.cdiv(nb, 8)
    bias1 = _top3_bias(s_sel[:sel_rows], qi)
    bias0 = _top3_bias(s_sel[half:half + sel_rows], qi)
    qs = qt * (SCALE * LOG2E)
    pad = jnp.zeros((half - sel_rows, tq), F32)
    zeros = jnp.zeros((half, tq), F32)
    q_past = (jnp.concatenate([qs[:half], bias0, pad], axis=0).astype(BF16),
              jnp.concatenate([bias1, pad, qs[half:]], axis=0).astype(BF16))
    q_own = (jnp.concatenate([qs[:half], zeros], axis=0).astype(BF16),
             jnp.concatenate([zeros, qs[half:]], axis=0).astype(BF16))

    def rows_of(gi):
        return pl.ds(pl.multiple_of(gi * grp, grp), grp)

    def qk(gi, slot):
        for hh in range(2):
            s_ref[slot, hh] = _dot(ka_ref[hh, rows_of(gi), :], q_past[hh])

    qk(0, 0)

    own = pl.ds(pl.multiple_of(qi * blk, blk), blk)
    causal = (lax.broadcasted_iota(jnp.int32, (blk, tq), 0) <= lax.broadcasted_iota(jnp.int32, (blk, tq), 1))
    state = []
    for hh in range(2):
        s = jnp.where(causal, _dot(ka_ref[hh, own, :], q_own[hh]), NEG_BIG)
        m = jnp.max(s, axis=0, keepdims=True)
        p = jnp.exp2(s - m)
        state += [m, jnp.sum(p, axis=0, keepdims=True),
                  _dot(vtb_ref[hh * half:(hh + 1) * half, own], p.astype(BF16))]

    def softmax_pv(gi, slot, carry):
        out = []
        for hh in range(2):
            m, l, acc = carry[3 * hh: 3 * hh + 3]
            s = s_ref[slot, hh]
            m_new = jnp.maximum(m, jnp.max(s, axis=0, keepdims=True))
            alpha = jnp.exp2(m - m_new)
            p = jnp.exp2(s - m_new)
            p_ref[slot, hh] = p.astype(BF16)
            out += [m_new, alpha * l + jnp.sum(p, axis=0, keepdims=True),
                    alpha * acc + _dot(vtb_ref[hh * half:(hh + 1) * half, rows_of(gi)], p_ref[slot, hh])]
        return tuple(out)

    last_group = nb // KEY_GROUP - 1

    def step(pi, carry):
        g0 = 2 * pi
        qk(g0 + 1, 1)
        carry = softmax_pv(g0, 0, carry)
        qk(jnp.minimum(g0 + 2, last_group), 0)
        return softmax_pv(g0 + 1, 1, carry)

    n_pairs = (qi + 2 * KEY_GROUP - 1) // (2 * KEY_GROUP)
    m0, l0, acc0, m1, l1, acc1 = lax.fori_loop(0, n_pairs, step, tuple(state))
    out_t = jnp.concatenate([acc0 / l0, acc1 / l1], axis=0).astype(BF16)
    eye = (lax.broadcasted_iota(jnp.int32, (tq, tq), 0)
           == lax.broadcasted_iota(jnp.int32, (tq, tq), 1)).astype(BF16)
    o_ref[...] = _dot_nt(eye, out_t).astype(o_ref.dtype)


def _moba_prompt(qt, k, vt_all, layer):
    batch, d, seq = qt.shape
    nb = seq // MOBA_BLOCK
    assert nb % (2 * KEY_GROUP) == 0 and 8 * pl.cdiv(nb, 8) <= HEAD_DIM
    tq = MOBA_BLOCK
    return pl.pallas_call(
        functools.partial(_moba_kernel, nb=nb),
        out_shape=jax.ShapeDtypeStruct((batch * seq, d), BF16),
        grid=(batch, d // LANES, nb),
        in_specs=[pl.BlockSpec((1, LANES, tq), lambda b, hp, qi: (b, hp, qi)),
                  pl.BlockSpec((seq, LANES), lambda b, hp, qi: (b, hp)),
                  pl.BlockSpec((1, 1, LANES, seq), lambda b, hp, qi: (layer, b, hp, 0))],
        out_specs=pl.BlockSpec((tq, LANES), lambda b, hp, qi: (b * nb + qi, hp)),
        scratch_shapes=[pltpu.VMEM((2, seq, LANES), BF16), pltpu.VMEM((LANES, seq), BF16),
                        pltpu.VMEM((LANES, LANES), F32),
                        pltpu.VMEM((2, 2, KEY_GROUP * MOBA_BLOCK, tq), F32),
                        pltpu.VMEM((2, 2, KEY_GROUP * MOBA_BLOCK, tq), BF16)],
        compiler_params=_params("parallel", "parallel", "arbitrary"),
        name="moba_prompt",
    )(qt, k, vt_all)


def _decode_kernel(*refs, n_pages, pages_per_block):
    pt_ref, q_ref, qt_ref, kn_ref, vnt_ref = refs[:5]
    kp_refs = refs[5:5 + n_pages]
    vp_refs = refs[5 + n_pages:5 + 2 * n_pages]
    ot_ref, s_ref, p_ref = refs[5 + 2 * n_pages:]
    del pt_ref
    nh, hd = q_ref.shape[1], q_ref.shape[2]
    page = s_ref.shape[2]
    nbk = n_pages // pages_per_block
    qt = qt_ref[0]

    for h in range(nh):
        qcol = jnp.broadcast_to(qt[:, h:h + 1], (hd, page))
        for j in range(n_pages):
            s_ref[j, h:h + 1, :] = jnp.sum(kp_refs[j][0, 0, h] * qcol, axis=0, keepdims=True)

    s_all = s_ref[...]
    tot = jnp.sum(s_all, axis=-1, keepdims=True)
    sc = jnp.concatenate(
        [sum(tot[n * pages_per_block + i: n * pages_per_block + i + 1] for i in range(pages_per_block))
         for n in range(nbk)], axis=0)
    nidx = lax.broadcasted_iota(jnp.int32, sc.shape, 0)
    bias = jnp.full(sc.shape, NEG_BIG, F32)
    for _ in range(MOBA_TOPK):
        mx = jnp.max(sc, axis=0, keepdims=True)
        first = jnp.min(jnp.where(sc == mx, nidx, 1 << 20), axis=0, keepdims=True)
        pick = nidx == first
        bias = jnp.where(pick, 0.0, bias)
        sc = jnp.where(pick, -jnp.inf, sc)
    bias_p = jnp.concatenate([bias[j // pages_per_block: j // pages_per_block + 1]
                              for j in range(n_pages)], axis=0)

    s_own = jnp.sum(q_ref[0] * kn_ref[0], axis=-1, keepdims=True) * SCALE
    sm = s_all * SCALE + bias_p
    m = jnp.max(jnp.max(sm, axis=0), axis=-1, keepdims=True)
    m = jnp.maximum(m, s_own)
    p = jnp.exp(sm - m[None])
    e_own = jnp.exp(s_own - m)
    l = jnp.sum(jnp.sum(p, axis=0), axis=-1, keepdims=True) + e_own
    p_ref[...] = p
    w_own = e_own / l
    inv_l = 1.0 / l
    vnt = vnt_ref[0]
    for h in range(nh):
        acc = jnp.zeros((hd, page), F32)
        for j in range(n_pages):
            acc = acc + vp_refs[j][0, 0, h] * p_ref[j, h:h + 1, :]
        col = jnp.sum(acc, axis=1, keepdims=True)
        ot_ref[0, :, h:h + 1] = col * inv_l[h:h + 1, :] + vnt[:, h:h + 1] * w_own[h:h + 1, :]


def _moba_decode(q, k_new, v_new, cache_kt, cache_vt, page_table, layer):
    db, n_pages = page_table.shape
    _, _, nh, hd, page = cache_kt.shape
    ppb = MOBA_BLOCK // page
    tok = pl.BlockSpec((1, nh, hd), lambda b, pt: (b, 0, 0))
    tok_t = pl.BlockSpec((1, hd, nh), lambda b, pt: (b, 0, 0))
    pages = [pl.BlockSpec((1, 1, nh, hd, page), lambda b, pt, j=j: (layer, pt[b, j], 0, 0, 0))
             for j in range(n_pages)]
    grid_spec = pltpu.PrefetchScalarGridSpec(
        num_scalar_prefetch=1,
        grid=(db,),
        in_specs=[tok, tok_t, tok, tok_t] + pages + pages,
        out_specs=tok_t,
        scratch_shapes=[pltpu.VMEM((n_pages, nh, page), F32), pltpu.VMEM((n_pages, nh, page), F32)],
    )
    swap = lambda t: jnp.swapaxes(t, 1, 2)
    ot = pl.pallas_call(
        functools.partial(_decode_kernel, n_pages=n_pages, pages_per_block=ppb),
        out_shape=jax.ShapeDtypeStruct((db, hd, nh), F32),
        grid_spec=grid_spec,
        compiler_params=_params("parallel"),
        name="moba_decode",
    )(page_table, q, swap(q), k_new, swap(v_new), *([cache_kt] * n_pages), *([cache_vt] * n_pages))
    return swap(ot)


def _gmlp_kernel(y_ref, g_ref, wuv_ref, lng_ref, lnb_ref, ws_ref, bs_ref, wo_ref, o_ref, *rest,
                 single_pos, vc):
    if single_pos:
        vout_ref, vbuf = rest
    else:
        (vbuf,) = rest
    tm = y_ref.shape[0]
    half = vbuf.shape[1]
    gd = half // GMLP_GROUPS
    y = y_ref[...]
    h = _rms(y, g_ref[...]).astype(BF16)
    for c in range(half // vc):
        vbuf[:, c * vc:(c + 1) * vc] = jax.nn.gelu(_dot(h, wuv_ref[:, half + c * vc: half + (c + 1) * vc]))
    v = vbuf[...]
    xc = v - jnp.mean(v, axis=-1, keepdims=True)
    vn = xc * lax.rsqrt(jnp.mean(xc * xc, axis=-1, keepdims=True) + EPS) * lng_ref[...] + lnb_ref[...]
    vbuf[...] = vn
    if single_pos:
        vout_ref[...] = vn
    else:
        tril = (lax.broadcasted_iota(jnp.int32, (CHUNK, CHUNK), 0)
                >= lax.broadcasted_iota(jnp.int32, (CHUNK, CHUNK), 1))
    for g in range(GMLP_GROUPS):
        cols = slice(g * gd, (g + 1) * gd)
        vg = vbuf[:, cols]
        if single_pos:
            vbuf[:, cols] = ws_ref[g, 0:1, 0:1] * vg + bs_ref[g, 0:1, :]
        else:
            wsg = jnp.where(tril, ws_ref[g], 0.0).astype(BF16)
            bsg = bs_ref[g]
            for r in range(tm // CHUNK):
                rs = slice(r * CHUNK, (r + 1) * CHUNK)
                vbuf[rs, cols] = _dot(wsg, vg[rs].astype(BF16)) + bsg
    acc = y
    for c in range(half // vc):
        cs = slice(c * vc, (c + 1) * vc)
        u = jax.nn.gelu(_dot(h, wuv_ref[:, cs]))
        acc = acc + _dot((u * vbuf[:, cs]).astype(BF16), wo_ref[cs, :])
    o_ref[...] = acc


def _gmlp(y, g, wuv, lng, lnb, ws, bs, wo, tm, single_pos):
    rows, d = y.shape
    half = wo.shape[0]
    row_spec = pl.BlockSpec((tm, d), lambda i: (i, 0))
    v_spec = pl.BlockSpec((tm, half), lambda i: (i, 0))
    out_shape = jax.ShapeDtypeStruct((rows, d), F32)
    out_specs = row_spec
    if single_pos:
        out_shape = (out_shape, jax.ShapeDtypeStruct((rows, half), F32))
        out_specs = (row_spec, v_spec)
    return pl.pallas_call(
        functools.partial(_gmlp_kernel, single_pos=single_pos, vc=512),
        out_shape=out_shape,
        grid=(rows // tm,),
        in_specs=[row_spec, _resident((1, d)), _resident(wuv.shape), _resident((1, half)),
                  _resident((1, half)), _resident(ws.shape), _resident(bs.shape), _resident(wo.shape)],
        out_specs=out_specs,
        scratch_shapes=[pltpu.VMEM((tm, half), F32)],
        compiler_params=_params("parallel"),
        name="gmlp_single" if single_pos else "gmlp",
    )(y, g, wuv, lng, lnb, ws, bs, wo)


def _ffn_kernel(*refs, has_proj, has_final, fc):
    it = iter(refs)
    y_ref = next(it)
    y = y_ref[...]
    if has_proj:
        a_ref, wo_ref = next(it), next(it)
        y = y + _dot(a_ref[...], wo_ref[...])
    g_ref, wup_ref, wdn_ref = next(it), next(it), next(it)
    gf_ref = next(it) if has_final else None
    o_ref = next(it)
    h = _rms(y, g_ref[...]).astype(BF16)
    acc = y
    for c in range(wup_ref.shape[1] // fc):
        u = jnp.square(jnp.maximum(_dot(h, wup_ref[:, c * fc:(c + 1) * fc]), 0.0)).astype(BF16)
        acc = acc + _dot(u, wdn_ref[c * fc:(c + 1) * fc, :])
    if has_final:
        acc = _rms(acc, gf_ref[...])
    o_ref[...] = acc


def _ffn(y, g, wup, wdn, tm, proj=None, final_g=None):
    rows, d = y.shape
    row_spec = pl.BlockSpec((tm, d), lambda i: (i, 0))
    args, specs = [y], [row_spec]
    if proj is not None:
        a, wo = proj
        args += [a, wo]
        specs += [pl.BlockSpec((tm, a.shape[1]), lambda i: (i, 0)), _resident(wo.shape)]
    args += [g, wup, wdn]
    specs += [_resident((1, d)), _resident(wup.shape), _resident(wdn.shape)]
    if final_g is not None:
        args.append(final_g)
        specs.append(_resident((1, d)))
    return pl.pallas_call(
        functools.partial(_ffn_kernel, has_proj=proj is not None, has_final=final_g is not None, fc=512),
        out_shape=jax.ShapeDtypeStruct((rows, d), F32),
        grid=(rows // tm,),
        in_specs=specs,
        out_specs=row_spec,
        compiler_params=_params("parallel"),
        name="ffn",
    )(*args)


def kernel(x_prompt, x_sample, cache_k, cache_v, page_table, norm_mix, norm_ffn, w_qkv, w_o_attn,
           w_uv, ln_v_g, ln_v_b, w_s, b_s, w_o_gmlp, w_up, w_down, norm_final):
    batch, seq, d = x_prompt.shape
    db, ds, _ = x_sample.shape
    depth = norm_mix.shape[0]
    assert ds == 1 and seq % MOBA_BLOCK == 0 and d == N_HEADS * HEAD_DIM
    assert (page_table.shape[1] * PAGE_SIZE) % MOBA_BLOCK == 0 and cache_k.shape[2] == PAGE_SIZE
    tm_p, tm_s = 512, db

    yp = x_prompt.reshape(batch * seq, d)
    ys = x_sample.reshape(db * ds, d)
    w_qkv_b, w_o_attn_b, w_uv_b = w_qkv.astype(BF16), w_o_attn.astype(BF16), w_uv.astype(BF16)
    w_o_gmlp_b, w_up_b, w_down_b = w_o_gmlp.astype(BF16), w_up.astype(BF16), w_down.astype(BF16)
    w_qkv_t = jnp.swapaxes(w_qkv_b, 1, 2)
    n_attn = w_qkv.shape[0]
    cache_kt = jnp.transpose(cache_k, (0, 1, 3, 4, 2))
    cache_vt = jnp.transpose(cache_v, (0, 1, 3, 4, 2))
    row = lambda t: t.reshape(1, -1)
    heads = lambda t: t.reshape(db, N_HEADS, HEAD_DIM)

    kv_t = None
    k_s, v_s, gv_s = [], [], []
    for i in range(depth):
        g_mix, g_ffn = row(norm_mix[i]), row(norm_ffn[i])
        final_g = row(norm_final) if i == depth - 1 else None
        if i % 2 == 0:
            a = i // 2
            qtp, ktp, vtp, kp = _qkv_t(yp, g_mix, w_qkv_t[a], w_qkv_b[a, :, d:2 * d], kv_t, a, n_attn,
                                       batch, seq, tm_p)
            kv_t = (ktp, vtp)
            qs, kn, vn = _qkv(ys, g_mix, w_qkv_b[a], tm_s)
            op = _moba_prompt(qtp, kp, vtp, a)
            os_ = _moba_decode(heads(qs), heads(kn), heads(vn), cache_kt, cache_vt, page_table, a)
            k_s.append(kn.reshape(db, ds, N_HEADS, HEAD_DIM))
            v_s.append(vn.reshape(db, ds, N_HEADS, HEAD_DIM))
            yp = _ffn(yp, g_ffn, w_up_b[i], w_down_b[i], tm_p, proj=(op, w_o_attn_b[a]), final_g=final_g)
            ys = _ffn(ys, g_ffn, w_up_b[i], w_down_b[i], tm_s,
                      proj=(os_.reshape(db, d).astype(BF16), w_o_attn_b[a]), final_g=final_g)
        else:
            gi = i // 2
            mix = (w_uv_b[gi], row(ln_v_g[gi]), row(ln_v_b[gi]), w_s[gi], b_s[gi][:, :, None], w_o_gmlp_b[gi])
            yp = _gmlp(yp, g_mix, *mix, tm=tm_p, single_pos=False)
            ys, vrows = _gmlp(ys, g_mix, *mix, tm=tm_s, single_pos=True)
            gv_s.append(vrows.reshape(db, ds, -1))
            yp = _ffn(yp, g_ffn, w_up_b[i], w_down_b[i], tm_p, final_g=final_g)
            ys = _ffn(ys, g_ffn, w_up_b[i], w_down_b[i], tm_s, final_g=final_g)
    untranspose = lambda t: jnp.transpose(t.reshape(n_attn, batch, N_HEADS, HEAD_DIM, seq), (0, 1, 4, 2, 3))
    return (yp.reshape(batch, seq, d), ys.reshape(db, ds, d), untranspose(kv_t[0]), untranspose(kv_t[1]),
            jnp.stack(k_s), jnp.stack(v_s), jnp.stack(gv_s))
```

```python
import functools

import jax
import jax.numpy as jnp
from jax import lax
from jax.experimental import pallas as pl
from jax.experimental.pallas import tpu as pltpu

F32 = jnp.float32
BF16 = jnp.bfloat16

N_HEADS = 16
HEAD_DIM = 64
MOBA_BLOCK = 256
MOBA_TOPK = 3
CHUNK = 128
GMLP_GROUPS = 8
PAGE_SIZE = 128
EPS = 1e-6
NEG_BIG = -1e30
LANES = 128
SCALE = HEAD_DIM ** -0.5
LOG2E = 1.4426950408889634
KEY_GROUP = 2

VMEM_LIMIT = 56 * 1024 * 1024


def _params(*sem):
    return pltpu.CompilerParams(dimension_semantics=sem, vmem_limit_bytes=VMEM_LIMIT)


def _resident(shape):
    nd = len(shape)
    return pl.BlockSpec(shape, lambda *_: (0,) * nd, pipeline_mode=pl.Buffered(1))


def _rms(x, g):
    return x * lax.rsqrt(jnp.mean(x * x, axis=-1, keepdims=True) + EPS) * g


def _dot(a, b, precision=None):
    return jnp.dot(a, b, preferred_element_type=F32, precision=precision)


def _dot_nt(a, b):
    return lax.dot_general(a, b, (((1,), (1,)), ((), ())), preferred_element_type=F32)


def _qkv_kernel(x_ref, g_ref, w_ref, q_ref, k_ref, v_ref, *, nc):
    d = x_ref.shape[1]
    h = _rms(x_ref[...], g_ref[...]).astype(BF16)
    for j, o_ref in enumerate((q_ref, k_ref, v_ref)):
        for c in range(d // nc):
            o_ref[:, c * nc:(c + 1) * nc] = _dot(h, w_ref[:, j * d + c * nc: j * d + (c + 1) * nc])


def _qkv(x, g, w, tm):
    rows, d = x.shape
    out = jax.ShapeDtypeStruct((rows, d), F32)
    row_spec = pl.BlockSpec((tm, d), lambda i: (i, 0))
    return pl.pallas_call(
        functools.partial(_qkv_kernel, nc=512),
        out_shape=(out, out, out),
        grid=(rows // tm,),
        in_specs=[row_spec, _resident((1, d)), _resident(w.shape)],
        out_specs=(row_spec, row_spec, row_spec),
        compiler_params=_params("parallel"),
        name="qkv",
    )(x, g, w)


def _qkv_t_kernel(x_ref, g_ref, wt_ref, wk_ref, *rest, nc):
    qt_ref, kt_ref, vt_ref, k_ref = rest[-4:]
    d = x_ref.shape[1]
    h = _rms(x_ref[...], g_ref[...]).astype(BF16)
    for c in range(d // nc):
        cs = slice(c * nc, (c + 1) * nc)
        qt_ref[0, cs, :] = _dot_nt(wt_ref[cs, :], h)
        kt_ref[0, 0, cs, :] = _dot_nt(wt_ref[d + c * nc: d + (c + 1) * nc, :], h)
        vt_ref[0, 0, cs, :] = _dot_nt(wt_ref[2 * d + c * nc: 2 * d + (c + 1) * nc, :], h)
        k_ref[:, cs] = _dot(h, wk_ref[:, cs])


def _qkv_t(x, g, wt, wk, kv_prev, layer, n_layers, batch, seq, tm):
    rows, d = x.shape
    per_batch = seq // tm
    row_spec = pl.BlockSpec((tm, d), lambda i: (i, 0))
    qt_spec = pl.BlockSpec((1, d, tm), lambda i: (i // per_batch, 0, i % per_batch))
    kv_spec = pl.BlockSpec((1, 1, d, tm), lambda i: (layer, i // per_batch, 0, i % per_batch))
    kv_shape = jax.ShapeDtypeStruct((n_layers, batch, d, seq), F32)
    args = [x, g, wt, wk]
    specs = [row_spec, _resident((1, d)), _resident(wt.shape), _resident(wk.shape)]
    aliases = {}
    if kv_prev is not None:
        args += list(kv_prev)
        specs += [pl.BlockSpec(memory_space=pl.ANY)] * 2
        aliases = {4: 1, 5: 2}
    return pl.pallas_call(
        functools.partial(_qkv_t_kernel, nc=512),
        out_shape=(jax.ShapeDtypeStruct((batch, d, seq), F32), kv_shape, kv_shape,
                   jax.ShapeDtypeStruct((rows, d), F32)),
        grid=(rows // tm,),
        in_specs=specs,
        out_specs=(qt_spec, kv_spec, kv_spec, row_spec),
        input_output_aliases=aliases,
        compiler_params=_params("parallel"),
        name="qkv_t",
    )(*args)


def _top3_bias(s, n_valid):
    ridx = lax.broadcasted_iota(jnp.int32, s.shape, 0)
    valid = ridx < n_valid
    s = jnp.where(valid, s, -jnp.inf)
    bias = jnp.full(s.shape, NEG_BIG, F32)
    for _ in range(MOBA_TOPK):
        mx = jnp.max(s, axis=0, keepdims=True)
        first = jnp.min(jnp.where(s == mx, ridx, 1 << 20), axis=0, keepdims=True)
        pick = ridx == first
        bias = jnp.where(pick, 0.0, bias)
        s = jnp.where(pick, -jnp.inf, s)
    return jnp.where(valid, bias, NEG_BIG)


def _moba_kernel(qt_ref, k_ref, vt_ref, o_ref, ka_ref, vtb_ref, kmt_ref, s_ref, p_ref, *, nb):
    tq = qt_ref.shape[2]
    blk = MOBA_BLOCK
    half = HEAD_DIM
    grp = KEY_GROUP * blk
    qi = pl.program_id(2)

    @pl.when(qi == 0)
    def _():
        kmt_ref[...] = jnp.zeros(kmt_ref.shape, F32)
        lane_b = lax.broadcasted_iota(jnp.int32, (blk, LANES), 1)
        lane1 = lax.broadcasted_iota(jnp.int32, (1, LANES), 1)

        def build(n, carry):
            rows = pl.ds(pl.multiple_of(n * blk, blk), blk)
            kb = k_ref[rows, :]
            ka_ref[0, rows, :] = jnp.where(lane_b < half, kb, (lane_b - half == n).astype(F32)).astype(BF16)
            ka_ref[1, rows, :] = jnp.where(lane_b >= half, kb, (lane_b == n).astype(F32)).astype(BF16)
            km = jnp.mean(kb, axis=0, keepdims=True)
            kmt_ref[pl.ds(n, 1), :] = jnp.where(lane1 >= half, km, 0.0)
            kmt_ref[pl.ds(half + n, 1), :] = jnp.where(lane1 < half, km, 0.0)
            return carry

        lax.fori_loop(0, nb, build, 0)
        for n in range(nb):
            vtb_ref[:, n * blk:(n + 1) * blk] = vt_ref[0, 0, :, n * blk:(n + 1) * blk].astype(BF16)

    qt = qt_ref[0]
    s_sel = _dot(kmt_ref[...], qt, precision=lax.Precision.HIGHEST)
    sel_rows = 8 * pl.cdiv(nb, 8)
    bias1 = _top3_bias(s_sel[:sel_rows], qi)
    bias0 = _top3_bias(s_sel[half:half + sel_rows], qi)
    qs = qt * (SCALE * LOG2E)
    pad = jnp.zeros((half - sel_rows, tq), F32)
    zeros = jnp.zeros((half, tq), F32)
    q_past = (jnp.concatenate([qs[:half], bias0, pad], axis=0).astype(BF16),
              jnp.concatenate([bias1, pad, qs[half:]], axis=0).astype(BF16))
    q_own = (jnp.concatenate([qs[:half], zeros], axis=0).astype(BF16),
             jnp.concatenate([zeros, qs[half:]], axis=0).astype(BF16))

    def rows_of(gi):
        return pl.ds(pl.multiple_of(gi * grp, grp), grp)

    def qk(gi, slot):
        for hh in range(2):
            s_ref[slot, hh] = _dot(ka_ref[hh, rows_of(gi), :], q_past[hh])

    own = pl.ds(pl.multiple_of(qi * blk, blk), blk)
    causal = (lax.broadcasted_iota(jnp.int32, (blk, tq), 0) <= lax.broadcasted_iota(jnp.int32, (blk, tq), 1))
    state = []
    for hh in range(2):
        s = jnp.where(causal, _dot(ka_ref[hh, own, :], q_own[hh]), NEG_BIG)
        m = jnp.max(s, axis=0, keepdims=True)
        p = jnp.exp2(s - m)
        state += [m, jnp.sum(p, axis=0, keepdims=True),
                  _dot(vtb_ref[hh * half:(hh + 1) * half, own], p.astype(BF16))]

    def softmax_pv(gi, slot, carry):
        out = []
        for hh in range(2):
            m, l, acc = carry[3 * hh: 3 * hh + 3]
            s = s_ref[slot, hh]
            m_new = jnp.maximum(m, jnp.max(s, axis=0, keepdims=True))
            alpha = jnp.exp2(m - m_new)
            p = jnp.exp2(s - m_new)
            p_ref[slot, hh] = p.astype(BF16)
            out += [m_new, alpha * l + jnp.sum(p, axis=0, keepdims=True),
                    alpha * acc + _dot(vtb_ref[hh * half:(hh + 1) * half, rows_of(gi)], p_ref[slot, hh])]
        return tuple(out)

    last_group = nb // KEY_GROUP - 1

    def step(pi, carry):
        g0 = 2 * pi
        qk(g0 + 1, 1)
        carry = softmax_pv(g0, 0, carry)
        qk(jnp.minimum(g0 + 2, last_group), 0)
        return softmax_pv(g0 + 1, 1, carry)

    qk(0, 0)
    n_pairs = (qi + 2 * KEY_GROUP - 1) // (2 * KEY_GROUP)
    m0, l0, acc0, m1, l1, acc1 = lax.fori_loop(0, n_pairs, step, tuple(state))
    out_t = jnp.concatenate([acc0 / l0, acc1 / l1], axis=0).astype(BF16)
    eye = (lax.broadcasted_iota(jnp.int32, (tq, tq), 0)
           == lax.broadcasted_iota(jnp.int32, (tq, tq), 1)).astype(BF16)
    o_ref[...] = _dot_nt(eye, out_t).astype(o_ref.dtype)


def _moba_prompt(qt, k, vt_all, layer):
    batch, d, seq = qt.shape
    nb = seq // MOBA_BLOCK
    assert nb % (2 * KEY_GROUP) == 0 and 8 * pl.cdiv(nb, 8) <= HEAD_DIM
    tq = MOBA_BLOCK
    return pl.pallas_call(
        functools.partial(_moba_kernel, nb=nb),
        out_shape=jax.ShapeDtypeStruct((batch * seq, d), BF16),
        grid=(batch, d // LANES, nb),
        in_specs=[pl.BlockSpec((1, LANES, tq), lambda b, hp, qi: (b, hp, qi)),
                  pl.BlockSpec((seq, LANES), lambda b, hp, qi: (b, hp)),
                  pl.BlockSpec((1, 1, LANES, seq), lambda b, hp, qi: (layer, b, hp, 0))],
        out_specs=pl.BlockSpec((tq, LANES), lambda b, hp, qi: (b * nb + qi, hp)),
        scratch_shapes=[pltpu.VMEM((2, seq, LANES), BF16), pltpu.VMEM((LANES, seq), BF16),
                        pltpu.VMEM((LANES, LANES), F32),
                        pltpu.VMEM((2, 2, KEY_GROUP * MOBA_BLOCK, tq), F32),
                        pltpu.VMEM((2, 2, KEY_GROUP * MOBA_BLOCK, tq), BF16)],
        compiler_params=_params("parallel", "parallel", "arbitrary"),
        name="moba_prompt",
    )(qt, k, vt_all)


def _decode_kernel(*refs, n_pages, pages_per_block):
    pt_ref, q_ref, qt_ref, kn_ref, vnt_ref = refs[:5]
    kp_refs = refs[5:5 + n_pages]
    vp_refs = refs[5 + n_pages:5 + 2 * n_pages]
    ot_ref, s_ref, p_ref = refs[5 + 2 * n_pages:]
    del pt_ref
    nh, hd = q_ref.shape[1], q_ref.shape[2]
    page = s_ref.shape[2]
    nbk = n_pages // pages_per_block
    qt = qt_ref[0]

    for h in range(nh):
        qcol = jnp.broadcast_to(qt[:, h:h + 1], (hd, page))
        for j in range(n_pages):
            s_ref[j, h:h + 1, :] = jnp.sum(kp_refs[j][0, 0, h] * qcol, axis=0, keepdims=True)

    s_all = s_ref[...]
    tot = jnp.sum(s_all, axis=-1, keepdims=True)
    sc = jnp.concatenate(
        [sum(tot[n * pages_per_block + i: n * pages_per_block + i + 1] for i in range(pages_per_block))
         for n in range(nbk)], axis=0)
    nidx = lax.broadcasted_iota(jnp.int32, sc.shape, 0)
    bias = jnp.full(sc.shape, NEG_BIG, F32)
    for _ in range(MOBA_TOPK):
        mx = jnp.max(sc, axis=0, keepdims=True)
        first = jnp.min(jnp.where(sc == mx, nidx, 1 << 20), axis=0, keepdims=True)
        pick = nidx == first
        bias = jnp.where(pick, 0.0, bias)
        sc = jnp.where(pick, -jnp.inf, sc)
    bias_p = jnp.concatenate([bias[j // pages_per_block: j // pages_per_block + 1]
                              for j in range(n_pages)], axis=0)

    s_own = jnp.sum(q_ref[0] * kn_ref[0], axis=-1, keepdims=True) * SCALE
    sm = s_all * SCALE + bias_p
    m = jnp.max(jnp.max(sm, axis=0), axis=-1, keepdims=True)
    m = jnp.maximum(m, s_own)
    p = jnp.exp(sm - m[None])
    e_own = jnp.exp(s_own - m)
    l = jnp.sum(jnp.sum(p, axis=0), axis=-1, keepdims=True) + e_own
    p_ref[...] = p
    w_own = e_own / l
    inv_l = 1.0 / l
    vnt = vnt_ref[0]
    for h in range(nh):
        acc = jnp.zeros((hd, page), F32)
        for j in range(n_pages):
            acc = acc + vp_refs[j][0, 0, h] * p_ref[j, h:h + 1, :]
        col = jnp.sum(acc, axis=1, keepdims=True)
        ot_ref[0, :, h:h + 1] = col * inv_l[h:h + 1, :] + vnt[:, h:h + 1] * w_own[h:h + 1, :]


def _moba_decode(q, k_new, v_new, cache_kt, cache_vt, page_table, layer):
    db, n_pages = page_table.shape
    _, _, nh, hd, page = cache_kt.shape
    ppb = MOBA_BLOCK // page
    tok = pl.BlockSpec((1, nh, hd), lambda b, pt: (b, 0, 0))
    tok_t = pl.BlockSpec((1, hd, nh), lambda b, pt: (b, 0, 0))
    pages = [pl.BlockSpec((1, 1, nh, hd, page), lambda b, pt, j=j: (layer, pt[b, j], 0, 0, 0))
             for j in range(n_pages)]
    grid_spec = pltpu.PrefetchScalarGridSpec(
        num_scalar_prefetch=1,
        grid=(db,),
        in_specs=[tok, tok_t, tok, tok_t] + pages + pages,
        out_specs=tok_t,
        scratch_shapes=[pltpu.VMEM((n_pages, nh, page), F32), pltpu.VMEM((n_pages, nh, page), F32)],
    )
    swap = lambda t: jnp.swapaxes(t, 1, 2)
    ot = pl.pallas_call(
        functools.partial(_decode_kernel, n_pages=n_pages, pages_per_block=ppb),
        out_shape=jax.ShapeDtypeStruct((db, hd, nh), F32),
        grid_spec=grid_spec,
        compiler_params=_params("parallel"),
        name="moba_decode",
    )(page_table, q, swap(q), k_new, swap(v_new), *([cache_kt] * n_pages), *([cache_vt] * n_pages))
    return swap(ot)


def _gmlp_kernel(y_ref, g_ref, wuv_ref, lng_ref, lnb_ref, ws_ref, bs_ref, wo_ref, o_ref, *rest,
                 single_pos, vc):
    if single_pos:
        vout_ref, vbuf = rest
    else:
        (vbuf,) = rest
    tm = y_ref.shape[0]
    half = vbuf.shape[1]
    gd = half // GMLP_GROUPS
    y = y_ref[...]
    h = _rms(y, g_ref[...]).astype(BF16)
    for c in range(half // vc):
        vbuf[:, c * vc:(c + 1) * vc] = jax.nn.gelu(_dot(h, wuv_ref[:, half + c * vc: half + (c + 1) * vc]))
    v = vbuf[...]
    xc = v - jnp.mean(v, axis=-1, keepdims=True)
    vn = xc * lax.rsqrt(jnp.mean(xc * xc, axis=-1, keepdims=True) + EPS) * lng_ref[...] + lnb_ref[...]
    vbuf[...] = vn
    if single_pos:
        vout_ref[...] = vn
    else:
        tril = (lax.broadcasted_iota(jnp.int32, (CHUNK, CHUNK), 0)
                >= lax.broadcasted_iota(jnp.int32, (CHUNK, CHUNK), 1))
    for g in range(GMLP_GROUPS):
        cols = slice(g * gd, (g + 1) * gd)
        vg = vbuf[:, cols]
        if single_pos:
            vbuf[:, cols] = ws_ref[g, 0:1, 0:1] * vg + bs_ref[g, 0:1, :]
        else:
            wsg = jnp.where(tril, ws_ref[g], 0.0).astype(BF16)
            bsg = bs_ref[g]
            for r in range(tm // CHUNK):
                rs = slice(r * CHUNK, (r + 1) * CHUNK)
                vbuf[rs, cols] = _dot(wsg, vg[rs].astype(BF16)) + bsg
    acc = y
    for c in range(half // vc):
        cs = slice(c * vc, (c + 1) * vc)
        u = jax.nn.gelu(_dot(h, wuv_ref[:, cs]))
        acc = acc + _dot((u * vbuf[:, cs]).astype(BF16), wo_ref[cs, :])
    o_ref[...] = acc


def _gmlp(y, g, wuv, lng, lnb, ws, bs, wo, tm, single_pos):
    rows, d = y.shape
    half = wo.shape[0]
    row_spec = pl.BlockSpec((tm, d), lambda i: (i, 0))
    v_spec = pl.BlockSpec((tm, half), lambda i: (i, 0))
    out_shape = jax.ShapeDtypeStruct((rows, d), F32)
    out_specs = row_spec
    if single_pos:
        out_shape = (out_shape, jax.ShapeDtypeStruct((rows, half), F32))
        out_specs = (row_spec, v_spec)
    return pl.pallas_call(
        functools.partial(_gmlp_kernel, single_pos=single_pos, vc=1024),
        out_shape=out_shape,
        grid=(rows // tm,),
        in_specs=[row_spec, _resident((1, d)), _resident(wuv.shape), _resident((1, half)),
                  _resident((1, half)), _resident(ws.shape), _resident(bs.shape), _resident(wo.shape)],
        out_specs=out_specs,
        scratch_shapes=[pltpu.VMEM((tm, half), F32)],
        compiler_params=_params("parallel"),
        name="gmlp_single" if single_pos else "gmlp",
    )(y, g, wuv, lng, lnb, ws, bs, wo)


def _ffn_kernel(*refs, has_proj, has_final, fc):
    it = iter(refs)
    y_ref = next(it)
    y = y_ref[...]
    if has_proj:
        a_ref, wo_ref = next(it), next(it)
        y = y + _dot(a_ref[...], wo_ref[...])
    g_ref, wup_ref, wdn_ref = next(it), next(it), next(it)
    gf_ref = next(it) if has_final else None
    o_ref = next(it)
    h = _rms(y, g_ref[...]).astype(BF16)
    acc = y
    for c in range(wup_ref.shape[1] // fc):
        u = jnp.square(jnp.maximum(_dot(h, wup_ref[:, c * fc:(c + 1) * fc]), 0.0)).astype(BF16)
        acc = acc + _dot(u, wdn_ref[c * fc:(c + 1) * fc, :])
    if has_final:
        acc = _rms(acc, gf_ref[...])
    o_ref[...] = acc


def _ffn(y, g, wup, wdn, tm, proj=None, final_g=None):
    rows, d = y.shape
    row_spec = pl.BlockSpec((tm, d), lambda i: (i, 0))
    args, specs = [y], [row_spec]
    if proj is not None:
        a, wo = proj
        args += [a, wo]
        specs += [pl.BlockSpec((tm, a.shape[1]), lambda i: (i, 0)), _resident(wo.shape)]
    args += [g, wup, wdn]
    specs += [_resident((1, d)), _resident(wup.shape), _resident(wdn.shape)]
    if final_g is not None:
        args.append(final_g)
        specs.append(_resident((1, d)))
    return pl.pallas_call(
        functools.partial(_ffn_kernel, has_proj=proj is not None, has_final=final_g is not None, fc=512),
        out_shape=jax.ShapeDtypeStruct((rows, d), F32),
        grid=(rows // tm,),
        in_specs=specs,
        out_specs=row_spec,
        compiler_params=_params("parallel"),
        name="ffn",
    )(*args)


def kernel(x_prompt, x_sample, cache_k, cache_v, page_table, norm_mix, norm_ffn, w_qkv, w_o_attn,
           w_uv, ln_v_g, ln_v_b, w_s, b_s, w_o_gmlp, w_up, w_down, norm_final):
    batch, seq, d = x_prompt.shape
    db, ds, _ = x_sample.shape
    depth = norm_mix.shape[0]
    assert ds == 1 and seq % MOBA_BLOCK == 0 and d == N_HEADS * HEAD_DIM
    assert (page_table.shape[1] * PAGE_SIZE) % MOBA_BLOCK == 0 and cache_k.shape[2] == PAGE_SIZE
    tm_p, tm_s = 512, db

    yp = x_prompt.reshape(batch * seq, d)
    ys = x_sample.reshape(db * ds, d)
    w_qkv_b, w_o_attn_b, w_uv_b = w_qkv.astype(BF16), w_o_attn.astype(BF16), w_uv.astype(BF16)
    w_o_gmlp_b, w_up_b, w_down_b = w_o_gmlp.astype(BF16), w_up.astype(BF16), w_down.astype(BF16)
    w_qkv_t = jnp.swapaxes(w_qkv_b, 1, 2)
    n_attn = w_qkv.shape[0]
    cache_kt = jnp.transpose(cache_k, (0, 1, 3, 4, 2))
    cache_vt = jnp.transpose(cache_v, (0, 1, 3, 4, 2))
    row = lambda t: t.reshape(1, -1)
    heads = lambda t: t.reshape(db, N_HEADS, HEAD_DIM)

    kv_t = None
    k_s, v_s, gv_s = [], [], []
    for i in range(depth):
        g_mix, g_ffn = row(norm_mix[i]), row(norm_ffn[i])
        final_g = row(norm_final) if i == depth - 1 else None
        if i % 2 == 0:
            a = i // 2
            qtp, ktp, vtp, kp = _qkv_t(yp, g_mix, w_qkv_t[a], w_qkv_b[a, :, d:2 * d], kv_t, a, n_attn,
                                       batch, seq, tm_p)
            kv_t = (ktp, vtp)
            qs, kn, vn = _qkv(ys, g_mix, w_qkv_b[a], tm_s)
            op = _moba_prompt(qtp, kp, vtp, a)
            os_ = _moba_decode(heads(qs), heads(kn), heads(vn), cache_kt, cache_vt, page_table, a)
            k_s.append(kn.reshape(db, ds, N_HEADS, HEAD_DIM))
            v_s.append(vn.reshape(db, ds, N_HEADS, HEAD_DIM))
            yp = _ffn(yp, g_ffn, w_up_b[i], w_down_b[i], tm_p, proj=(op, w_o_attn_b[a]), final_g=final_g)
            ys = _ffn(ys, g_ffn, w_up_b[i], w_down_b[i], tm_s,
                      proj=(os_.reshape(db, d).astype(BF16), w_o_attn_b[a]), final_g=final_g)
        else:
            gi = i // 2
            mix = (w_uv_b[gi], row(ln_v_g[gi]), row(ln_v_b[gi]), w_s[gi], b_s[gi][:, :, None], w_o_gmlp_b[gi])
            yp = _gmlp(yp, g_mix, *mix, tm=tm_p, single_pos=False)
            ys, vrows = _gmlp(ys, g_mix, *mix, tm=tm_s, single_pos=True)
            gv_s.append(vrows.reshape(db, ds, -1))
            yp = _ffn(yp, g_ffn, w_up_b[i], w_down_b[i], tm_p, final_g=final_g)
            ys = _ffn(ys, g_ffn, w_up_b[i], w_down_b[i], tm_s, final_g=final_g)
    untranspose = lambda t: jnp.transpose(t.reshape(n_attn, batch, N_HEADS, HEAD_DIM, seq), (0, 1, 4, 2, 3))
    return (yp.reshape(batch, seq, d), ys.reshape(db, ds, d), untranspose(kv_t[0]), untranspose(kv_t[1]),
            jnp.stack(k_s), jnp.stack(v_s), jnp.stack(gv_s))
```
